```python
import math
import jax, jax.numpy as jnp
from jax import lax
import numpy as np

D_MODEL = 2048
BATCH = 2
SEQ = 16384
DEPTH = 2

CHUNK = 64
RMS_EPS = 1e-6
D_FF = 5632
RWKV_HEADS = 12
RWKV_HEAD_DIM = 64
RWKV_DIM = RWKV_HEADS * RWKV_HEAD_DIM
DECAY_LORA = 64
ICLR_LORA = 64
GATE_LORA = 128
RWKV_GN_EPS = 64e-5
GDN_HEADS = 6
GDN_HEAD_DIM = 128
GDN_DIM = GDN_HEADS * GDN_HEAD_DIM
GDN_CONV = 4
LRU_BLOCKS = 8
LRU_BLOCK_DIM = 64
LRU_DIM = LRU_BLOCKS * LRU_BLOCK_DIM
LRU_CONV = 4
LRU_C = 8.0
MIX_DIM = RWKV_DIM + GDN_DIM + LRU_DIM
RWKV_IN = 3 * RWKV_DIM + DECAY_LORA + ICLR_LORA + GATE_LORA
GDN_IN = 4 * GDN_DIM + 2 * GDN_HEADS
LRU_IN = 2 * LRU_DIM
PROJ_DIM = RWKV_IN + GDN_IN + LRU_IN

kernel_name = "hybrid_rwkv7_gdn_rglru_macaron"


def rmsnorm(x, g):
    xf = x.astype(jnp.float32)
    y = xf * lax.rsqrt(jnp.mean(xf * xf, axis=-1, keepdims=True) + RMS_EPS)
    return (y * g.astype(jnp.float32)).astype(x.dtype)


def l2norm(x):
    xf = x.astype(jnp.float32)
    return xf * lax.rsqrt(jnp.sum(xf * xf, axis=-1, keepdims=True) + 1e-12)


def swiglu(h, wg, wu, wd):
    return (jax.nn.silu(h @ wg) * (h @ wu)) @ wd


def causal_dwconv(x, w):
    K, C = w.shape
    return lax.conv_general_dilated(x, w[:, None, :].astype(x.dtype), window_strides=(1,),
                                    padding=[(K - 1, 0)],
                                    dimension_numbers=('NWC', 'WIO', 'NWC'),
                                    feature_group_count=C)


def rwkv7_recurrence(r, w, k, v, kk, a):
    B, T, H, N = r.shape

    def step(S, inp):
        r_t, w_t, k_t, v_t, kk_t, a_t = inp
        sa = jnp.einsum('bhvk,bhk->bhv', S, kk_t)
        S = (S * w_t[:, :, None, :]
             - sa[..., None] * (kk_t * a_t)[:, :, None, :]
             + v_t[..., None] * k_t[:, :, None, :])
        return S, jnp.einsum('bhvk,bhk->bhv', S, r_t)

    xs = tuple(jnp.swapaxes(t, 0, 1) for t in (r, w, k, v, kk, a))
    _, y = lax.scan(step, jnp.zeros((B, H, N, N), jnp.float32), xs)
    return jnp.swapaxes(y, 0, 1)


def rwkv7_mixer(z, mu, w0, w2, a0, a2, g2, k_k, k_a, r_k, ln_g, ln_b):
    B, T, _ = z.shape
    zf = z.astype(jnp.float32)
    z_prev = jnp.pad(zf[:, :-1], ((0, 0), (1, 0), (0, 0)))
    zf = zf + mu * (z_prev - zf)
    o1 = RWKV_DIM
    o2 = 2 * RWKV_DIM
    o3 = 3 * RWKV_DIM
    o4 = o3 + DECAY_LORA
    o5 = o4 + ICLR_LORA
    r, k, v, zw, za, zg = jnp.split(zf, [o1, o2, o3, o4, o5], axis=-1)
    w_log = -jax.nn.softplus(-(w0 + jnp.tanh(zw) @ w2)) - 0.5
    decay = jnp.exp(-jnp.exp(w_log))
    a = jax.nn.sigmoid(a0 + za @ a2)
    g = jax.nn.sigmoid(zg) @ g2
    heads = lambda t: t.reshape(B, T, RWKV_HEADS, RWKV_HEAD_DIM)
    kk = l2norm(heads(k * k_k))
    k = k * (1.0 + (a - 1.0) * k_a)
    r, k, v, decay, a = map(heads, (r, k, v, decay, a))
    y = rwkv7_recurrence(r, decay, k, v, kk, a)
    mean = jnp.mean(y, axis=-1, keepdims=True)
    var = jnp.mean(jnp.square(y - mean), axis=-1, keepdims=True)
    y = ((y - mean) * lax.rsqrt(var + RWKV_GN_EPS)).reshape(B, T, RWKV_DIM) * ln_g + ln_b
    bonus = jnp.sum(r * k * r_k, axis=-1, keepdims=True) * v
    y = (y + bonus.reshape(B, T, RWKV_DIM)) * g
    return y.astype(z.dtype)


def gated_delta_chunked(q, k, v, g, beta):
    B, T, H, DK = q.shape
    DV = v.shape[-1]
    n = T // CHUNK

    def blocks(t):
        t = t.reshape((B, n, CHUNK, H) + t.shape[3:])
        return jnp.moveaxis(t, 3, 1)

    q, k, v, g, beta = map(blocks, (q, k, v, g, beta))
    gc = jnp.cumsum(g, axis=-1)
    idx = jnp.arange(CHUNK)
    causal = idx[:, None] >= idx[None, :]
    strict = idx[:, None] > idx[None, :]
    diff = gc[..., :, None] - gc[..., None, :]
    decay = jnp.where(causal, jnp.exp(jnp.where(causal, diff, 0.0)), 0.0)
    kb = k * beta[..., None]
    m = jnp.where(strict, jnp.einsum('bhnid,bhnjd->bhnij', kb, k) * decay, 0.0)
    lower = m + jnp.eye(CHUNK, dtype=m.dtype)
    rhs = jnp.concatenate([v * beta[..., None], kb * jnp.exp(gc)[..., None]], axis=-1)
    sol = lax.linalg.triangular_solve(lower, rhs, left_side=True, lower=True,
                                      unit_diagonal=True)
    u, wk = sol[..., :DV], sol[..., DV:]
    qk = jnp.where(causal, jnp.einsum('bhnid,bhnjd->bhnij', q, k) * decay, 0.0)

    def step(S, inp):
        q_i, k_i, u_i, w_i, qk_i, gc_i = inp
        v_new = u_i - jnp.einsum('bhcd,bhde->bhce', w_i, S)
        o = (jnp.einsum('bhcd,bhde->bhce', q_i * jnp.exp(gc_i)[..., None], S)
             + jnp.einsum('bhij,bhje->bhie', qk_i, v_new))
        g_last = gc_i[..., -1:]
        S = (S * jnp.exp(g_last)[..., None]
             + jnp.einsum('bhcd,bhce->bhde', k_i * jnp.exp(g_last - gc_i)[..., None], v_new))
        return S, o

    xs = tuple(jnp.moveaxis(t, 2, 0) for t in (q, k, u, wk, qk, gc))
    _, o = lax.scan(step, jnp.zeros((B, H, DK, DV), jnp.float32), xs)
    o = jnp.transpose(o, (1, 0, 3, 2, 4))
    return o.reshape(B, T, H, DV)


def gdn_mixer(z, conv_w, a_log, dt_bias, norm_g):
    B, T, _ = z.shape
    qkv, gate, zb, za = jnp.split(z, [3 * GDN_DIM, 4 * GDN_DIM, 4 * GDN_DIM + GDN_HEADS], axis=-1)
    qkv = jax.nn.silu(causal_dwconv(qkv, conv_w)).astype(jnp.float32)
    q, k, v = jnp.split(qkv.reshape(B, T, 3 * GDN_HEADS, GDN_HEAD_DIM), 3, axis=2)
    q = l2norm(q) * (GDN_HEAD_DIM ** -0.5)
    k = l2norm(k)
    beta = jax.nn.sigmoid(zb.astype(jnp.float32))
    g = -jnp.exp(a_log.astype(jnp.float32)) * jax.nn.softplus(za.astype(jnp.float32) + dt_bias)
    o = gated_delta_chunked(q, k, v, g, beta)
    o = rmsnorm(o, norm_g) * jax.nn.silu(gate.astype(jnp.float32).reshape(B, T, GDN_HEADS, GDN_HEAD_DIM))
    return o.reshape(B, T, GDN_DIM).astype(z.dtype)


def rglru_mixer(z, conv_w, conv_b, w_a, b_a, w_x, b_x, lam):
    B, T, _ = z.shape
    xl, yl = jnp.split(z, 2, axis=-1)
    xc = (causal_dwconv(xl, conv_w) + conv_b).astype(jnp.float32)
    xb = xc.reshape(B, T, LRU_BLOCKS, LRU_BLOCK_DIM)
    r = jax.nn.sigmoid(jnp.einsum('btnc,ncd->btnd', xb, w_a).reshape(B, T, LRU_DIM) + b_a)
    i = jax.nn.sigmoid(jnp.einsum('btnc,ncd->btnd', xb, w_x).reshape(B, T, LRU_DIM) + b_x)
    log_a = -LRU_C * r * jax.nn.softplus(-lam.astype(jnp.float32))
    a = jnp.exp(log_a)
    u = jnp.sqrt(-jnp.expm1(2.0 * log_a)) * (i * xc)

    def combine(e, l):
        return (e[0] * l[0], l[0] * e[1] + l[1])

    _, h = lax.associative_scan(combine, (a, u), axis=1)
    return (h * jax.nn.gelu(yl.astype(jnp.float32))).astype(z.dtype)


def setup_inputs(seed: int = 0) -> dict:
    key = jax.random.key(seed)
    ks = iter(jax.random.split(key, 48))
    f32 = jnp.float32
    L = DEPTH

    def nrm(shape, scale):
        return jax.random.normal(next(ks), shape, f32) * scale

    def gain(shape):
        return 1.0 + nrm(shape, 0.02)

    def unif(shape, lo, hi):
        return jax.random.uniform(next(ks), shape, f32, lo, hi)

    x = nrm((BATCH, SEQ, D_MODEL), 1.0)
    norm1_g = gain((L, D_MODEL))
    ffn1_wg = nrm((L, D_MODEL, D_FF), D_MODEL ** -0.5)
    ffn1_wu = nrm((L, D_MODEL, D_FF), D_MODEL ** -0.5)
    ffn1_wd = nrm((L, D_FF, D_MODEL), D_FF ** -0.5)
    norm_mix_g = gain((L, D_MODEL))
    w_in = nrm((L, D_MODEL, PROJ_DIM), D_MODEL ** -0.5)
    rw_mu = unif((L, RWKV_IN), 0.0, 1.0)
    rw_w0 = unif((L, RWKV_DIM), -6.5, -1.5)
    rw_w2 = nrm((L, DECAY_LORA, RWKV_DIM), 0.5 * DECAY_LORA ** -0.5)
    rw_a0 = nrm((L, RWKV_DIM), 0.1)
    rw_a2 = nrm((L, ICLR_LORA, RWKV_DIM), 0.5 * ICLR_LORA ** -0.5)
    rw_g2 = nrm((L, GATE_LORA, RWKV_DIM), GATE_LORA ** -0.5)
    rw_kk = 0.85 + nrm((L, RWKV_DIM), 0.02)
    rw_ka = gain((L, RWKV_DIM))
    rw_rk = nrm((L, RWKV_HEADS, RWKV_HEAD_DIM), 0.1)
    rw_ln_g = gain((L, RWKV_DIM))
    rw_ln_b = nrm((L, RWKV_DIM), 0.02)
    gd_conv_w = nrm((L, GDN_CONV, 3 * GDN_DIM), GDN_CONV ** -0.5)
    gd_a_log = jnp.log(unif((L, GDN_HEADS), 1.0, 16.0))
    dt = jnp.exp(unif((L, GDN_HEADS), math.log(1e-3), math.log(1e-1)))
    gd_dt_bias = dt + jnp.log(-jnp.expm1(-dt))
    gd_norm_g = gain((L, GDN_HEAD_DIM))
    lr_conv_w = nrm((L, LRU_CONV, LRU_DIM), LRU_CONV ** -0.5)
    lr_conv_b = nrm((L, LRU_DIM), 0.02)
    lr_wa = nrm((L, LRU_BLOCKS, LRU_BLOCK_DIM, LRU_BLOCK_DIM), LRU_BLOCK_DIM ** -0.5)
    lr_ba = nrm((L, LRU_DIM), 0.02)
    lr_wx = nrm((L, LRU_BLOCKS, LRU_BLOCK_DIM, LRU_BLOCK_DIM), LRU_BLOCK_DIM ** -0.5)
    lr_bx = nrm((L, LRU_DIM), 0.02)
    s = unif((L, LRU_DIM), 0.9, 0.999) ** (1.0 / LRU_C)
    lr_lam = jnp.log(s) - jnp.log1p(-s)
    w_out = nrm((L, MIX_DIM, D_MODEL), MIX_DIM ** -0.5)
    norm2_g = gain((L, D_MODEL))
    ffn2_wg = nrm((L, D_MODEL, D_FF), D_MODEL ** -0.5)
    ffn2_wu = nrm((L, D_MODEL, D_FF), D_MODEL ** -0.5)
    ffn2_wd = nrm((L, D_FF, D_MODEL), D_FF ** -0.5)
    final_g = gain((D_MODEL,))
    return {"x": x, "norm1_g": norm1_g, "ffn1_wg": ffn1_wg, "ffn1_wu": ffn1_wu,
            "ffn1_wd": ffn1_wd, "norm_mix_g": norm_mix_g, "w_in": w_in,
            "rw_mu": rw_mu, "rw_w0": rw_w0, "rw_w2": rw_w2, "rw_a0": rw_a0,
            "rw_a2": rw_a2, "rw_g2": rw_g2, "rw_kk": rw_kk, "rw_ka": rw_ka,
            "rw_rk": rw_rk, "rw_ln_g": rw_ln_g, "rw_ln_b": rw_ln_b,
            "gd_conv_w": gd_conv_w, "gd_a_log": gd_a_log, "gd_dt_bias": gd_dt_bias,
            "gd_norm_g": gd_norm_g, "lr_conv_w": lr_conv_w, "lr_conv_b": lr_conv_b,
            "lr_wa": lr_wa, "lr_ba": lr_ba, "lr_wx": lr_wx, "lr_bx": lr_bx,
            "lr_lam": lr_lam, "w_out": w_out, "norm2_g": norm2_g,
            "ffn2_wg": ffn2_wg, "ffn2_wu": ffn2_wu, "ffn2_wd": ffn2_wd,
            "final_g": final_g}


def reference(x, norm1_g, ffn1_wg, ffn1_wu, ffn1_wd, norm_mix_g, w_in,
              rw_mu, rw_w0, rw_w2, rw_a0, rw_a2, rw_g2, rw_kk, rw_ka, rw_rk,
              rw_ln_g, rw_ln_b, gd_conv_w, gd_a_log, gd_dt_bias, gd_norm_g,
              lr_conv_w, lr_conv_b, lr_wa, lr_ba, lr_wx, lr_bx, lr_lam, w_out,
              norm2_g, ffn2_wg, ffn2_wu, ffn2_wd, final_g):
    for l in range(DEPTH):
        x = x + 0.5 * swiglu(rmsnorm(x, norm1_g[l]), ffn1_wg[l], ffn1_wu[l], ffn1_wd[l])
        h = rmsnorm(x, norm_mix_g[l])
        z = h @ w_in[l]
        z_rw, z_gd, z_lr = jnp.split(z, [RWKV_IN, RWKV_IN + GDN_IN], axis=-1)
        y_rw = rwkv7_mixer(z_rw, rw_mu[l], rw_w0[l], rw_w2[l], rw_a0[l], rw_a2[l], rw_g2[l],
                           rw_kk[l], rw_ka[l], rw_rk[l], rw_ln_g[l], rw_ln_b[l])
        y_gd = gdn_mixer(z_gd, gd_conv_w[l], gd_a_log[l], gd_dt_bias[l], gd_norm_g[l])
        y_lr = rglru_mixer(z_lr, lr_conv_w[l], lr_conv_b[l], lr_wa[l], lr_ba[l],
                           lr_wx[l], lr_bx[l], lr_lam[l])
        mix = jnp.concatenate([y_rw, y_gd, y_lr], axis=-1)
        x = x + mix @ w_out[l]
        x = x + 0.5 * swiglu(rmsnorm(x, norm2_g[l]), ffn2_wg[l], ffn2_wu[l], ffn2_wd[l])
    return rmsnorm(x, final_g)
```

```python
import functools

import jax
import jax.numpy as jnp
from jax import lax
from jax.experimental import pallas as pl
from jax.experimental.pallas import tpu as pltpu

F32 = jnp.float32
BF16 = jnp.bfloat16

D_MODEL = 2048
DEPTH = 2
CHUNK = 64
RMS_EPS = 1e-6
D_FF = 5632
RWKV_HEADS = 12
RWKV_HEAD_DIM = 64
RWKV_DIM = RWKV_HEADS * RWKV_HEAD_DIM
DECAY_LORA = 64
ICLR_LORA = 64
GATE_LORA = 128
RWKV_GN_EPS = 64e-5
GDN_HEADS = 6
GDN_HEAD_DIM = 128
GDN_DIM = GDN_HEADS * GDN_HEAD_DIM
GDN_CONV = 4
LRU_BLOCKS = 8
LRU_BLOCK_DIM = 64
LRU_DIM = LRU_BLOCKS * LRU_BLOCK_DIM
LRU_CONV = 4
LRU_C = 8.0
RWKV_IN = 3 * RWKV_DIM + DECAY_LORA + ICLR_LORA + GATE_LORA
GDN_IN = 4 * GDN_DIM + 2 * GDN_HEADS
LRU_IN = 2 * LRU_DIM

V7X_LANES = 128
V7X_SUBLANES = 8
V7X_VMEM_BYTES = 64 * 1024 * 1024
VMEM_LIMIT_BYTES = V7X_VMEM_BYTES - 8 * 1024 * 1024

GDN_SMALL = V7X_LANES
GDN_Z = 4 * GDN_DIM + GDN_SMALL
HALO = V7X_SUBLANES


def _params(sem):
    return pltpu.CompilerParams(dimension_semantics=sem, vmem_limit_bytes=VMEM_LIMIT_BYTES)


def _dot(a, b, precision=None):
    return jnp.dot(a, b, preferred_element_type=F32, precision=precision)


def _dot_nt(a, b):
    return lax.dot_general(a, b, (((1,), (1,)), ((), ())), preferred_element_type=F32)


def _silu(x):
    return x * jax.nn.sigmoid(x)


def _softplus(x):
    return jnp.maximum(x, 0.0) + jnp.log1p(jnp.exp(-jnp.abs(x)))


def _rms(x, g):
    return x * lax.rsqrt(jnp.mean(x * x, axis=-1, keepdims=True) + RMS_EPS) * g


def _ffn_kernel(*refs, nf, final):
    if final:
        x_ref, g_ref, wg_ref, wu_ref, wd_ref, fg_ref, o_ref, h_ref = refs
    else:
        x_ref, g_ref, wg_ref, wu_ref, wd_ref, o_ref, h_ref = refs
    j = pl.program_id(1)

    @pl.when(j == 0)
    def _():
        h_ref[...] = _rms(x_ref[...], g_ref[...]).astype(BF16)
        o_ref[...] = jnp.zeros_like(o_ref)

    h = h_ref[...]
    a = _dot(h, wg_ref[...])
    u = _dot(h, wu_ref[...])
    act = (_silu(a) * u).astype(BF16)
    o_ref[...] += _dot(act, wd_ref[...])

    @pl.when(j == nf - 1)
    def _():
        y = x_ref[...] + 0.5 * o_ref[...]
        if final:
            y = _rms(y, fg_ref[...])
        o_ref[...] = y


def _ffn(x, g, wg, wu, wd, final_g=None, *, tm=512, tf=512):
    n, d = x.shape
    f = wg.shape[1]
    final = final_g is not None
    row = lambda i, j: (0, 0)
    in_specs = [
        pl.BlockSpec((tm, d), lambda i, j: (i, 0)),
        pl.BlockSpec((1, d), row),
        pl.BlockSpec((d, tf), lambda i, j: (0, j)),
        pl.BlockSpec((d, tf), lambda i, j: (0, j)),
        pl.BlockSpec((tf, d), lambda i, j: (j, 0)),
    ]
    args = [x, g.reshape(1, d), wg, wu, wd]
    if final:
        in_specs.append(pl.BlockSpec((1, d), row))
        args.append(final_g.reshape(1, d))
    return pl.pallas_call(
        functools.partial(_ffn_kernel, nf=f // tf, final=final),
        grid=(n // tm, f // tf),
        in_specs=in_specs,
        out_specs=pl.BlockSpec((tm, d), lambda i, j: (i, 0)),
        out_shape=jax.ShapeDtypeStruct((n, d), F32),
        scratch_shapes=[pltpu.VMEM((tm, d), BF16)],
        compiler_params=_params(("parallel", "arbitrary")),
        name="ffn",
    )(*args)


def _norm_proj_kernel(x_ref, g_ref, w_ref, o_ref):
    h = _rms(x_ref[...], g_ref[...]).astype(BF16)
    o_ref[...] = _dot(h, w_ref[...])


def _norm_proj(x, g, w, *, tm=512, name):
    n, d = x.shape
    c = w.shape[1]
    return pl.pallas_call(
        _norm_proj_kernel,
        grid=(n // tm,),
        in_specs=[
            pl.BlockSpec((tm, d), lambda i: (i, 0)),
            pl.BlockSpec((1, d), lambda i: (0, 0)),
            pl.BlockSpec((d, c), lambda i: (0, 0)),
        ],
        out_specs=pl.BlockSpec((tm, c), lambda i: (i, 0)),
        out_shape=jax.ShapeDtypeStruct((n, c), F32),
        compiler_params=_params(("parallel",)),
        name=name,
    )(x, g.reshape(1, d), w)


def _out_proj_kernel(x_ref, yr_ref, yg_ref, yl_ref, wr_ref, wg_ref, wl_ref, o_ref):
    o_ref[...] = (x_ref[...] + _dot(yr_ref[...], wr_ref[...]) + _dot(yg_ref[...], wg_ref[...])
                  + _dot(yl_ref[...], wl_ref[...]))


def _out_proj(x, y_rw, y_gd, y_lr, w_rw, w_gd, w_lr, *, tm=512):
    n, d = x.shape
    tile = lambda c: pl.BlockSpec((tm, c), lambda i: (i, 0))
    whole = lambda c: pl.BlockSpec((c, d), lambda i: (0, 0))
    return pl.pallas_call(
        _out_proj_kernel,
        grid=(n // tm,),
        in_specs=[tile(d), tile(RWKV_DIM), tile(GDN_DIM), tile(LRU_DIM),
                  whole(RWKV_DIM), whole(GDN_DIM), whole(LRU_DIM)],
        out_specs=tile(d),
        out_shape=jax.ShapeDtypeStruct((n, d), F32),
        compiler_params=_params(("parallel",)),
        name="out_proj",
    )(x, y_rw, y_gd, y_lr, w_rw, w_gd, w_lr)


def _shift_rows(x, d, fill, row):
    return jnp.where(row >= d, pltpu.roll(x, d, 0), fill)


def _cumsum_rows(x, row):
    d = 1
    while d < x.shape[0]:
        x = x + _shift_rows(x, d, 0.0, row)
        d *= 2
    return x


def _block_diag(m, bd_mask):
    return jnp.where(bd_mask, jnp.concatenate([m, m], axis=0), 0.0)


def _pair_inverse(x, eye2, bd_mask):
    hi = lax.Precision.HIGHEST
    t = eye2 + x
    p = x
    steps = CHUNK.bit_length() - 2
    for _ in range(steps):
        p = _dot(p, _block_diag(p, bd_mask), hi)
        t = t + _dot(t, _block_diag(p, bd_mask), hi)
    return t


def _seg_sum(x, ones_bd):
    hi = x.astype(BF16)
    lo = (x - hi.astype(F32)).astype(BF16)
    return _dot(hi, ones_bd) + _dot(lo, ones_bd)


def _pair_masks():
    c = CHUNK
    row = lax.broadcasted_iota(jnp.int32, (c, 2 * c), 0)
    lane = lax.broadcasted_iota(jnp.int32, (c, 2 * c), 1)
    col = jnp.where(lane >= c, lane - c, lane)
    r2 = lax.broadcasted_iota(jnp.int32, (2 * c, 2 * c), 0)
    l2 = lax.broadcasted_iota(jnp.int32, (2 * c, 2 * c), 1)
    bd_mask = (r2 >= c) == (l2 >= c)
    return dict(row=row, first=lane < c, strict=row > col, incl=row >= col,
                eye=(row == col).astype(F32), bd=bd_mask)


def _lru_kernel(z_ref, cw_ref, cb_ref, wa_ref, ba_ref, wx_ref, bx_ref, lam_ref, o_ref,
                xbuf_ref, h_ref, *, tt):
    @pl.when(pl.program_id(1) == 0)
    def _():
        xbuf_ref[0:HALO, :] = jnp.zeros((HALO, LRU_DIM), F32)
        h_ref[...] = jnp.zeros_like(h_ref)

    xl = z_ref[:, 0:LRU_DIM]
    yl = z_ref[:, LRU_DIM:2 * LRU_DIM]
    xbuf_ref[HALO:HALO + tt, :] = xl
    xc = cb_ref[...] + cw_ref[LRU_CONV - 1:LRU_CONV, :] * xl
    for j in range(1, LRU_CONV):
        xc = xc + cw_ref[LRU_CONV - 1 - j:LRU_CONV - j, :] * xbuf_ref[HALO - j:HALO - j + tt, :]
    xbuf_ref[0:HALO, :] = xbuf_ref[tt:tt + HALO, :]

    xcb = xc.astype(BF16)
    half = LRU_DIM // 2
    ra = jnp.concatenate([_dot(xcb[:, p * half:(p + 1) * half], wa_ref[p]) for p in range(2)], axis=1)
    ia = jnp.concatenate([_dot(xcb[:, p * half:(p + 1) * half], wx_ref[p]) for p in range(2)], axis=1)
    r = jax.nn.sigmoid(ra + ba_ref[...])
    i = jax.nn.sigmoid(ia + bx_ref[...])
    log_a = -LRU_C * r * _softplus(-lam_ref[...])
    a = jnp.exp(log_a)
    u = jnp.sqrt(1.0 - a * a) * (i * xc)

    row = lax.broadcasted_iota(jnp.int32, (tt, LRU_DIM), 0)
    d = 1
    while d < tt:
        u = a * _shift_rows(u, d, 0.0, row) + u
        a = a * _shift_rows(a, d, 1.0, row)
        d *= 2
    h = u + a * h_ref[...]
    h_ref[...] = h[tt - 1:tt, :]
    o_ref[...] = (h * jax.nn.gelu(yl)).astype(o_ref.dtype)


def _lru_mixer(z, conv_w, conv_b, wa_bd, b_a, wx_bd, b_x, lam, *, tt=256):
    b, t, _ = z.shape
    vec = lambda a: a.reshape(1, LRU_DIM)
    const2 = lambda shape: pl.BlockSpec(shape, lambda i, j: (0, 0))
    const3 = lambda shape: pl.BlockSpec(shape, lambda i, j: (0, 0, 0))
    half = LRU_DIM // 2
    return pl.pallas_call(
        functools.partial(_lru_kernel, tt=tt),
        grid=(b, t // tt),
        in_specs=[pl.BlockSpec((None, tt, LRU_IN), lambda i, j: (i, j, 0)),
                  const2((LRU_CONV, LRU_DIM)), const2((1, LRU_DIM)),
                  const3((2, half, half)), const2((1, LRU_DIM)),
                  const3((2, half, half)), const2((1, LRU_DIM)), const2((1, LRU_DIM))],
        out_specs=pl.BlockSpec((None, tt, LRU_DIM), lambda i, j: (i, j, 0)),
        out_shape=jax.ShapeDtypeStruct((b, t, LRU_DIM), BF16),
        scratch_shapes=[pltpu.VMEM((tt + HALO, LRU_DIM), F32), pltpu.VMEM((1, LRU_DIM), F32)],
        compiler_params=_params(("parallel", "arbitrary")),
        name="rglru",
    )(z, conv_w, vec(conv_b), wa_bd, vec(b_a), wx_bd, vec(b_x), vec(lam))


def _gdn_kernel(z_ref, cw_ref, alog_ref, dtb_ref, ng_ref, o_ref,
                xbuf_ref, q_ref, k_ref, v_ref, beta_ref, g_ref, s_ref, *, tt):
    c = CHUNK
    hd = GDN_HEAD_DIM
    qkv_dim = 3 * GDN_DIM

    @pl.when(pl.program_id(1) == 0)
    def _():
        xbuf_ref[0:HALO, :] = jnp.zeros((HALO, qkv_dim), F32)
        s_ref[...] = jnp.zeros_like(s_ref)

    xbuf_ref[HALO:HALO + tt, :] = z_ref[:, 0:qkv_dim]
    for s in range(3 * GDN_HEADS):
        sl = slice(s * hd, (s + 1) * hd)
        acc = cw_ref[GDN_CONV - 1:GDN_CONV, sl] * xbuf_ref[HALO:HALO + tt, sl]
        for j in range(1, GDN_CONV):
            acc = acc + cw_ref[GDN_CONV - 1 - j:GDN_CONV - j, sl] * xbuf_ref[HALO - j:HALO - j + tt, sl]
        y = _silu(acc)
        which, head = divmod(s, GDN_HEADS)
        if which < 2:
            y = y * lax.rsqrt(jnp.sum(y * y, axis=-1, keepdims=True) + 1e-12)
        if which == 0:
            y = y * (hd ** -0.5)
        (q_ref, k_ref, v_ref)[which][:, head * hd:(head + 1) * hd] = y
    xbuf_ref[0:HALO, :] = xbuf_ref[tt:tt + HALO, :]

    small = z_ref[:, 4 * GDN_DIM:4 * GDN_DIM + GDN_SMALL]
    beta_ref[...] = jax.nn.sigmoid(small)
    g_ref[...] = -jnp.exp(alog_ref[...]) * _softplus(small + dtb_ref[...])

    m = _pair_masks()
    row_s = lax.broadcasted_iota(jnp.int32, (c, GDN_SMALL), 0)
    ng = ng_ref[...]

    def chunk(ci, carry):
        rows = pl.ds(pl.multiple_of(ci * c, c), c)
        gc = _cumsum_rows(g_ref[rows, :], row_s)
        gct = gc.T
        beta = beta_ref[rows, :]
        for p in range(GDN_HEADS // 2):
            h0, h1 = 2 * p, 2 * p + 1
            sl0 = slice(h0 * hd, (h0 + 1) * hd)
            sl1 = slice(h1 * hd, (h1 + 1) * hd)
            gcol = [gc[:, GDN_HEADS + h:GDN_HEADS + h + 1] for h in (h0, h1)]
            grow = [gct[GDN_HEADS + h:GDN_HEADS + h + 1, :] for h in (h0, h1)]
            bcol = [beta[:, h:h + 1] for h in (h0, h1)]
            gcol2 = jnp.where(m["first"], gcol[0], gcol[1])
            grow2 = jnp.concatenate(grow, axis=1)
            decay = jnp.where(m["incl"], jnp.exp(jnp.where(m["incl"], gcol2 - grow2, 0.0)), 0.0)
            q = [q_ref[rows, sl] for sl in (sl0, sl1)]
            k = [k_ref[rows, sl] for sl in (sl0, sl1)]
            v = [v_ref[rows, sl] for sl in (sl0, sl1)]
            kb = [k[i] * bcol[i] for i in range(2)]
            zero = jnp.zeros((c, hd), F32)
            lhs = jnp.concatenate([jnp.concatenate(kb, axis=1), jnp.concatenate(q, axis=1)], axis=0)
            rhs = jnp.concatenate([jnp.concatenate([k[0], zero], axis=1),
                                   jnp.concatenate([zero, k[1]], axis=1)], axis=0)
            kq = _dot_nt(lhs.astype(BF16), rhs.astype(BF16))
            mm = jnp.where(m["strict"], kq[0:c, :] * decay, 0.0)
            qk = kq[c:2 * c, :] * decay
            t = _pair_inverse(-mm, m["eye"], m["bd"])
            egc = [jnp.exp(gcol[i]) for i in range(2)]
            rhs_sol = jnp.concatenate(
                [jnp.concatenate([v[i] * bcol[i], kb[i] * egc[i]], axis=1) for i in range(2)],
                axis=0).astype(BF16)
            v_new = []
            o = []
            for i, h in enumerate((h0, h1)):
                sel = m["first"] if i == 0 else jnp.logical_not(m["first"])
                sol = _dot(jnp.where(sel, t, 0.0).astype(BF16), rhs_sol)
                u, wk = sol[:, 0:hd], sol[:, hd:2 * hd]
                s_mat = s_ref[h]
                s_b = s_mat.astype(BF16)
                vn = u - _dot(wk.astype(BF16), s_b)
                v_new.append(vn)
                o.append(_dot((q[i] * egc[i]).astype(BF16), s_b))
                g_last = gcol[i][c - 1:c, :]
                kd = (k[i] * jnp.exp(g_last - gcol[i])).T
                s_ref[h] = s_mat * jnp.exp(g_last) + _dot(kd.astype(BF16), vn.astype(BF16))
            vn2 = jnp.concatenate(v_new, axis=0).astype(BF16)
            for i, h in enumerate((h0, h1)):
                sel = m["first"] if i == 0 else jnp.logical_not(m["first"])
                oo = o[i] + _dot(jnp.where(sel, qk, 0.0).astype(BF16), vn2)
                sl = slice(h * hd, (h + 1) * hd)
                gate = z_ref[rows, 3 * GDN_DIM + h * hd:3 * GDN_DIM + (h + 1) * hd]
                o_ref[rows, sl] = (_rms(oo, ng) * _silu(gate)).astype(o_ref.dtype)
        return carry

    lax.fori_loop(0, tt // c, chunk, 0)


def _gdn_mixer(z, conv_w, a_log_pad, dt_bias_pad, norm_g, *, tt=256):
    b, t, _ = z.shape
    const2 = lambda shape: pl.BlockSpec(shape, lambda i, j: (0, 0))
    return pl.pallas_call(
        functools.partial(_gdn_kernel, tt=tt),
        grid=(b, t // tt),
        in_specs=[pl.BlockSpec((None, tt, GDN_Z), lambda i, j: (i, j, 0)),
                  const2((GDN_CONV, 3 * GDN_DIM)), const2((1, GDN_SMALL)), const2((1, GDN_SMALL)),
                  const2((1, GDN_HEAD_DIM))],
        out_specs=pl.BlockSpec((None, tt, GDN_DIM), lambda i, j: (i, j, 0)),
        out_shape=jax.ShapeDtypeStruct((b, t, GDN_DIM), BF16),
        scratch_shapes=[pltpu.VMEM((tt + HALO, 3 * GDN_DIM), F32),
                        pltpu.VMEM((tt, GDN_DIM), F32), pltpu.VMEM((tt, GDN_DIM), F32),
                        pltpu.VMEM((tt, GDN_DIM), F32),
                        pltpu.VMEM((tt, GDN_SMALL), F32), pltpu.VMEM((tt, GDN_SMALL), F32),
                        pltpu.VMEM((GDN_HEADS, GDN_HEAD_DIM, GDN_HEAD_DIM), F32)],
        compiler_params=_params(("parallel", "arbitrary")),
        name="gdn",
    )(z, conv_w, a_log_pad, dt_bias_pad, norm_g.reshape(1, GDN_HEAD_DIM))


def _rwkv_kernel(z_ref, mu_ref, w0_ref, w2_ref, a0_ref, a2_ref, g2_ref, kk_ref, ka_ref, rk_ref,
                 lng_ref, lnb_ref, o_ref,
                 zbuf_ref, r_ref, k_ref, v_ref, n_ref, a_ref, lw_ref, y_ref, p_ref, *, tt):
    c = CHUNK
    w = V7X_LANES
    n_slabs = RWKV_DIM // w

    @pl.when(pl.program_id(1) == 0)
    def _():
        zbuf_ref[0:HALO, :] = jnp.zeros((HALO, RWKV_IN), F32)
        p_ref[...] = jnp.zeros_like(p_ref)

    m = _pair_masks()
    ones_bd = jnp.where(m["bd"], 1.0, 0.0).astype(BF16)
    not_first = jnp.logical_not(m["first"])

    zbuf_ref[HALO:HALO + tt, :] = z_ref[...]

    def shifted(lo, hi):
        cur = zbuf_ref[HALO:HALO + tt, lo:hi]
        prev = zbuf_ref[HALO - 1:HALO - 1 + tt, lo:hi]
        return cur + mu_ref[:, lo:hi] * (prev - cur)

    lora = shifted(3 * RWKV_DIM, 3 * RWKV_DIM + DECAY_LORA + ICLR_LORA)
    lora_t = jnp.tanh(lora).astype(BF16)
    lora_b = lora.astype(BF16)
    gate_in = jax.nn.sigmoid(shifted(RWKV_IN - GATE_LORA, RWKV_IN)).astype(BF16)
    for s in range(n_slabs):
        sl = slice(s * w, (s + 1) * w)
        r = shifted(s * w, (s + 1) * w)
        k = shifted(RWKV_DIM + s * w, RWKV_DIM + (s + 1) * w)
        v = shifted(2 * RWKV_DIM + s * w, 2 * RWKV_DIM + (s + 1) * w)
        w_log = -_softplus(-(w0_ref[:, sl] + _dot(lora_t, w2_ref[:, sl]))) - 0.5
        a = jax.nn.sigmoid(a0_ref[:, sl] + _dot(lora_b, a2_ref[:, sl]))
        kn = k * kk_ref[:, sl]
        kn = kn * lax.rsqrt(_seg_sum(kn * kn, ones_bd) + 1e-12)
        r_ref[:, sl] = r
        k_ref[:, sl] = k * (1.0 + (a - 1.0) * ka_ref[:, sl])
        v_ref[:, sl] = v
        n_ref[:, sl] = kn
        a_ref[:, sl] = a
        lw_ref[:, sl] = -jnp.exp(w_log)
    zbuf_ref[0:HALO, :] = zbuf_ref[tt:tt + HALO, :]

    def chunk(ci, carry):
        rows = pl.ds(pl.multiple_of(ci * c, c), c)
        for s in range(n_slabs):
            sl = slice(s * w, (s + 1) * w)
            r, k, v = r_ref[rows, sl], k_ref[rows, sl], v_ref[rows, sl]
            kn, a, lw = n_ref[rows, sl], a_ref[rows, sl], lw_ref[rows, sl]
            cum = _cumsum_rows(lw, m["row"])
            e_out = jnp.exp(-cum)
            b = kn * a
            at = -kn * jnp.exp(cum - lw)
            rt = r * jnp.exp(cum)
            bt = b * e_out
            kt = k * e_out
            d_end = jnp.exp(cum[c - 1:c, :] - cum)
            lhs = jnp.concatenate([at, rt], axis=0).astype(BF16)
            rhs = jnp.concatenate([jnp.where(m["first"], bt, 0.0), jnp.where(not_first, bt, 0.0),
                                   jnp.where(m["first"], kt, 0.0), jnp.where(not_first, kt, 0.0)],
                                  axis=0).astype(BF16)
            aa = _dot_nt(lhs, rhs)
            a_ab = jnp.where(m["strict"], aa[0:c, 0:w], 0.0)
            a_ak = jnp.where(m["strict"], aa[0:c, w:2 * w], 0.0)
            a_rb = jnp.where(m["incl"], aa[c:2 * c, 0:w], 0.0)
            a_rk = jnp.where(m["incl"], aa[c:2 * c, w:2 * w], 0.0)
            t = _pair_inverse(a_ab, m["eye"], m["bd"]).astype(BF16)
            v_bd = _block_diag(v, m["bd"]).astype(BF16)
            akv = _dot(a_ak.astype(BF16), v_bd)
            tu = _dot(t, jnp.concatenate([_block_diag(akv, m["bd"]), _block_diag(at, m["bd"])],
                                         axis=1).astype(BF16))
            p_mat = p_ref[s]
            p_b = p_mat.astype(BF16)
            u = tu[:, 0:w] + _dot(tu[:, w:2 * w].astype(BF16), p_b)
            y = (_dot(rt.astype(BF16), p_b)
                 + _dot(a_rb.astype(BF16), _block_diag(u, m["bd"]).astype(BF16))
                 + _dot(a_rk.astype(BF16), v_bd))
            y_ref[rows, sl] = y
            kb_t = jnp.concatenate([k * d_end, b * d_end], axis=0).T
            upd = _dot(kb_t.astype(BF16), jnp.concatenate([v, u], axis=0).astype(BF16))
            d_col = jnp.exp(cum.T[:, c - 1:c])
            p_ref[s] = d_col * p_mat + jnp.where(m["bd"], upd, 0.0)
        return carry

    lax.fori_loop(0, tt // c, chunk, 0)

    inv_n = 1.0 / RWKV_HEAD_DIM
    for s in range(n_slabs):
        sl = slice(s * w, (s + 1) * w)
        y = y_ref[:, sl]
        mean = _seg_sum(y, ones_bd) * inv_n
        yc = y - mean
        var = _seg_sum(yc * yc, ones_bd) * inv_n
        y = yc * lax.rsqrt(var + RWKV_GN_EPS) * lng_ref[:, sl] + lnb_ref[:, sl]
        bonus = _seg_sum(r_ref[:, sl] * k_ref[:, sl] * rk_ref[:, sl], ones_bd) * v_ref[:, sl]
        g = _dot(gate_in, g2_ref[:, sl])
        o_ref[:, sl] = ((y + bonus) * g).astype(o_ref.dtype)


def _rwkv_mixer(z, mu, w0, w2_pad, a0, a2_pad, g2, k_k, k_a, r_k, ln_g, ln_b, *, tt=256):
    b, t, _ = z.shape
    lora = DECAY_LORA + ICLR_LORA
    const2 = lambda shape: pl.BlockSpec(shape, lambda i, j: (0, 0))
    vec = lambda a: a.reshape(1, RWKV_DIM)
    tile = pltpu.VMEM((tt, RWKV_DIM), F32)
    return pl.pallas_call(
        functools.partial(_rwkv_kernel, tt=tt),
        grid=(b, t // tt),
        in_specs=[pl.BlockSpec((None, tt, RWKV_IN), lambda i, j: (i, j, 0)),
                  const2((1, RWKV_IN)), const2((1, RWKV_DIM)), const2((lora, RWKV_DIM)),
                  const2((1, RWKV_DIM)), const2((lora, RWKV_DIM)), const2((GATE_LORA, RWKV_DIM))]
                 + [const2((1, RWKV_DIM))] * 5,
        out_specs=pl.BlockSpec((None, tt, RWKV_DIM), lambda i, j: (i, j, 0)),
        out_shape=jax.ShapeDtypeStruct((b, t, RWKV_DIM), BF16),
        scratch_shapes=[pltpu.VMEM((tt + HALO, RWKV_IN), F32)] + [tile] * 7
                       + [pltpu.VMEM((RWKV_DIM // V7X_LANES, V7X_LANES, V7X_LANES), F32)],
        compiler_params=_params(("parallel", "arbitrary")),
        name="rwkv7",
    )(z, mu.reshape(1, RWKV_IN), vec(w0), w2_pad, vec(a0), a2_pad, g2, vec(k_k), vec(k_a),
      vec(r_k), vec(ln_g), vec(ln_b))


def _block_diag_weights(w):
    per = (LRU_DIM // 2) // LRU_BLOCK_DIM
    out = jnp.zeros((2, LRU_DIM // 2, LRU_DIM // 2), F32)
    for blk in range(LRU_BLOCKS):
        p, i = divmod(blk, per)
        lo = i * LRU_BLOCK_DIM
        out = out.at[p, lo:lo + LRU_BLOCK_DIM, lo:lo + LRU_BLOCK_DIM].set(w[blk])
    return out.astype(BF16)


def kernel(x, norm1_g, ffn1_wg, ffn1_wu, ffn1_wd, norm_mix_g, w_in, rw_mu, rw_w0, rw_w2, rw_a0, rw_a2, rw_g2, rw_kk, rw_ka, rw_rk, rw_ln_g, rw_ln_b, gd_conv_w, gd_a_log, gd_dt_bias, gd_norm_g, lr_conv_w, lr_conv_b, lr_wa, lr_ba, lr_wx, lr_bx, lr_lam, w_out, norm2_g, ffn2_wg, ffn2_wu, ffn2_wd, final_g):
    b, t, d = x.shape
    n = b * t
    bf = lambda a: a.astype(BF16)
    xs = x.reshape(n, d)
    for l in range(DEPTH):
        xs = _ffn(xs, norm1_g[l], bf(ffn1_wg[l]), bf(ffn1_wu[l]), bf(ffn1_wd[l]))

        w_l = w_in[l]
        w_rw = bf(w_l[:, :RWKV_IN])
        w_gd = bf(jnp.pad(w_l[:, RWKV_IN:RWKV_IN + GDN_IN], ((0, 0), (0, GDN_Z - GDN_IN))))
        w_lr = bf(w_l[:, RWKV_IN + GDN_IN:])
        z_rw = _norm_proj(xs, norm_mix_g[l], w_rw, name="in_proj_rwkv").reshape(b, t, RWKV_IN)
        z_gd = _norm_proj(xs, norm_mix_g[l], w_gd, name="in_proj_gdn").reshape(b, t, GDN_Z)
        z_lr = _norm_proj(xs, norm_mix_g[l], w_lr, name="in_proj_lru").reshape(b, t, LRU_IN)

        w2_pad = bf(jnp.concatenate([rw_w2[l], jnp.zeros((ICLR_LORA, RWKV_DIM), F32)], axis=0))
        a2_pad = bf(jnp.concatenate([jnp.zeros((DECAY_LORA, RWKV_DIM), F32), rw_a2[l]], axis=0))
        y_rw = _rwkv_mixer(z_rw, rw_mu[l], rw_w0[l], w2_pad, rw_a0[l], a2_pad, bf(rw_g2[l]),
                           rw_kk[l], rw_ka[l], rw_rk[l].reshape(RWKV_DIM), rw_ln_g[l], rw_ln_b[l])

        lane_pad = lambda a: jnp.pad(a, (GDN_HEADS, GDN_SMALL - 2 * GDN_HEADS)).reshape(1, GDN_SMALL)
        y_gd = _gdn_mixer(z_gd, gd_conv_w[l], lane_pad(gd_a_log[l]), lane_pad(gd_dt_bias[l]),
                          gd_norm_g[l])

        y_lr = _lru_mixer(z_lr, lr_conv_w[l], lr_conv_b[l], _block_diag_weights(lr_wa[l]), lr_ba[l],
                          _block_diag_weights(lr_wx[l]), lr_bx[l], lr_lam[l])

        w_o = bf(w_out[l])
        xs = _out_proj(xs, y_rw.reshape(n, RWKV_DIM), y_gd.reshape(n, GDN_DIM),
                       y_lr.reshape(n, LRU_DIM), w_o[:RWKV_DIM], w_o[RWKV_DIM:RWKV_DIM + GDN_DIM],
                       w_o[RWKV_DIM + GDN_DIM:])
        last = l == DEPTH - 1
        xs = _ffn(xs, norm2_g[l], bf(ffn2_wg[l]), bf(ffn2_wu[l]), bf(ffn2_wd[l]),
                  final_g if last else None)
    return xs.reshape(b, t, d)
```

```python
import functools

import jax
import jax.numpy as jnp
from jax import lax
from jax.experimental import pallas as pl
from jax.experimental.pallas import tpu as pltpu

F32 = jnp.float32
BF16 = jnp.bfloat16

D_MODEL = 2048
DEPTH = 2
CHUNK = 64
RMS_EPS = 1e-6
D_FF = 5632
RWKV_HEADS = 12
RWKV_HEAD_DIM = 64
RWKV_DIM = RWKV_HEADS * RWKV_HEAD_DIM
DECAY_LORA = 64
ICLR_LORA = 64
GATE_LORA = 128
RWKV_GN_EPS = 64e-5
GDN_HEADS = 6
GDN_HEAD_DIM = 128
GDN_DIM = GDN_HEADS * GDN_HEAD_DIM
GDN_CONV = 4
LRU_BLOCKS = 8
LRU_BLOCK_DIM = 64
LRU_DIM = LRU_BLOCKS * LRU_BLOCK_DIM
LRU_CONV = 4
LRU_C = 8.0
RWKV_IN = 3 * RWKV_DIM + DECAY_LORA + ICLR_LORA + GATE_LORA
GDN_IN = 4 * GDN_DIM + 2 * GDN_HEADS
LRU_IN = 2 * LRU_DIM

V7X_LANES = 128
V7X_SUBLANES = 8
V7X_VMEM_BYTES = 64 * 1024 * 1024
VMEM_LIMIT_BYTES = V7X_VMEM_BYTES - 8 * 1024 * 1024

GDN_SMALL = V7X_LANES
GDN_Z = 4 * GDN_DIM + GDN_SMALL
HALO = V7X_SUBLANES


def _params(sem):
    return pltpu.CompilerParams(dimension_semantics=sem, vmem_limit_bytes=VMEM_LIMIT_BYTES)


def _dot(a, b, precision=None):
    return jnp.dot(a, b, preferred_element_type=F32, precision=precision)


def _dot_nt(a, b):
    return lax.dot_general(a, b, (((1,), (1,)), ((), ())), preferred_element_type=F32)


def _silu(x):
    return x * jax.nn.sigmoid(x)


def _softplus(x):
    return jnp.maximum(x, 0.0) + jnp.log1p(jnp.exp(-jnp.abs(x)))


def _rms(x, g):
    return x * lax.rsqrt(jnp.mean(x * x, axis=-1, keepdims=True) + RMS_EPS) * g


def _ffn_kernel(*refs, nf, final):
    if final:
        x_ref, g_ref, wg_ref, wu_ref, wd_ref, fg_ref, o_ref, h_ref = refs
    else:
        x_ref, g_ref, wg_ref, wu_ref, wd_ref, o_ref, h_ref = refs
    j = pl.program_id(1)

    @pl.when(j == 0)
    def _():
        h_ref[...] = _rms(x_ref[...], g_ref[...]).astype(BF16)
        o_ref[...] = jnp.zeros_like(o_ref)

    h = h_ref[...]
    a = _dot(h, wg_ref[...])
    u = _dot(h, wu_ref[...])
    act = (_silu(a) * u).astype(BF16)
    o_ref[...] += _dot(act, wd_ref[...])

    @pl.when(j == nf - 1)
    def _():
        y = x_ref[...] + 0.5 * o_ref[...]
        if final:
            y = _rms(y, fg_ref[...])
        o_ref[...] = y


def _ffn(x, g, wg, wu, wd, final_g=None, *, tm=512, tf=512):
    n, d = x.shape
    f = wg.shape[1]
    final = final_g is not None
    row = lambda i, j: (0, 0)
    in_specs = [
        pl.BlockSpec((tm, d), lambda i, j: (i, 0)),
        pl.BlockSpec((1, d), row),
        pl.BlockSpec((d, tf), lambda i, j: (0, j)),
        pl.BlockSpec((d, tf), lambda i, j: (0, j)),
        pl.BlockSpec((tf, d), lambda i, j: (j, 0)),
    ]
    args = [x, g.reshape(1, d), wg, wu, wd]
    if final:
        in_specs.append(pl.BlockSpec((1, d), row))
        args.append(final_g.reshape(1, d))
    return pl.pallas_call(
        functools.partial(_ffn_kernel, nf=f // tf, final=final),
        grid=(n // tm, f // tf),
        in_specs=in_specs,
        out_specs=pl.BlockSpec((tm, d), lambda i, j: (i, 0)),
        out_shape=jax.ShapeDtypeStruct((n, d), F32),
        scratch_shapes=[pltpu.VMEM((tm, d), BF16)],
        compiler_params=_params(("parallel", "arbitrary")),
        name="ffn",
    )(*args)


def _norm_proj_kernel(x_ref, g_ref, w_ref, o_ref):
    h = _rms(x_ref[...], g_ref[...]).astype(BF16)
    o_ref[...] = _dot(h, w_ref[...])


def _norm_proj(x, g, w, *, tm=512, name):
    n, d = x.shape
    c = w.shape[1]
    return pl.pallas_call(
        _norm_proj_kernel,
        grid=(n // tm,),
        in_specs=[
            pl.BlockSpec((tm, d), lambda i: (i, 0)),
            pl.BlockSpec((1, d), lambda i: (0, 0)),
            pl.BlockSpec((d, c), lambda i: (0, 0)),
        ],
        out_specs=pl.BlockSpec((tm, c), lambda i: (i, 0)),
        out_shape=jax.ShapeDtypeStruct((n, c), F32),
        compiler_params=_params(("parallel",)),
        name=name,
    )(x, g.reshape(1, d), w)


def _out_proj_kernel(x_ref, yr_ref, yg_ref, yl_ref, wr_ref, wg_ref, wl_ref, o_ref):
    o_ref[...] = (x_ref[...] + _dot(yr_ref[...], wr_ref[...]) + _dot(yg_ref[...], wg_ref[...])
                  + _dot(yl_ref[...], wl_ref[...]))


def _out_proj(x, y_rw, y_gd, y_lr, w_rw, w_gd, w_lr, *, tm=512):
    n, d = x.shape
    tile = lambda c: pl.BlockSpec((tm, c), lambda i: (i, 0))
    whole = lambda c: pl.BlockSpec((c, d), lambda i: (0, 0))
    return pl.pallas_call(
        _out_proj_kernel,
        grid=(n // tm,),
        in_specs=[tile(d), tile(RWKV_DIM), tile(GDN_DIM), tile(LRU_DIM),
                  whole(RWKV_DIM), whole(GDN_DIM), whole(LRU_DIM)],
        out_specs=tile(d),
        out_shape=jax.ShapeDtypeStruct((n, d), F32),
        compiler_params=_params(("parallel",)),
        name="out_proj",
    )(x, y_rw, y_gd, y_lr, w_rw, w_gd, w_lr)


def _shift_rows(x, d, fill, row):
    return jnp.where(row >= d, pltpu.roll(x, d, 0), fill)


def _cumsum_rows(x, row):
    d = 1
    while d < x.shape[0]:
        x = x + _shift_rows(x, d, 0.0, row)
        d *= 2
    return x


def _block_diag(m, bd_mask):
    return jnp.where(bd_mask, jnp.concatenate([m, m], axis=0), 0.0)


def _pair_inverse_many(xs, eye2, bd_mask):
    bd = lambda x: _block_diag(x, bd_mask).astype(BF16)
    ts = [eye2 + x for x in xs]
    ps = list(xs)
    pbs = [bd(p) for p in ps]
    steps = CHUNK.bit_length() - 2
    for _ in range(steps):
        ps = [_dot(p.astype(BF16), pb) for p, pb in zip(ps, pbs)]
        pbs = [bd(p) for p in ps]
        ts = [t + _dot(t.astype(BF16), pb) for t, pb in zip(ts, pbs)]
    return ts


def _seg_sum(x, ones_bd):
    hi = x.astype(BF16)
    lo = (x - hi.astype(F32)).astype(BF16)
    return _dot(hi, ones_bd) + _dot(lo, ones_bd)


def _pair_masks():
    c = CHUNK
    row = lax.broadcasted_iota(jnp.int32, (c, 2 * c), 0)
    lane = lax.broadcasted_iota(jnp.int32, (c, 2 * c), 1)
    col = jnp.where(lane >= c, lane - c, lane)
    r2 = lax.broadcasted_iota(jnp.int32, (2 * c, 2 * c), 0)
    l2 = lax.broadcasted_iota(jnp.int32, (2 * c, 2 * c), 1)
    bd_mask = (r2 >= c) == (l2 >= c)
    return dict(row=row, first=lane < c, strict=row > col, incl=row >= col,
                eye=(row == col).astype(F32), bd=bd_mask)


def _lru_kernel(z_ref, cw_ref, cb_ref, wa_ref, ba_ref, wx_ref, bx_ref, lam_ref, o_ref,
                xbuf_ref, h_ref, *, tt):
    @pl.when(pl.program_id(1) == 0)
    def _():
        xbuf_ref[0:HALO, :] = jnp.zeros((HALO, LRU_DIM), F32)
        h_ref[...] = jnp.zeros_like(h_ref)

    xl = z_ref[:, 0:LRU_DIM]
    yl = z_ref[:, LRU_DIM:2 * LRU_DIM]
    xbuf_ref[HALO:HALO + tt, :] = xl
    xc = cb_ref[...] + cw_ref[LRU_CONV - 1:LRU_CONV, :] * xl
    for j in range(1, LRU_CONV):
        xc = xc + cw_ref[LRU_CONV - 1 - j:LRU_CONV - j, :] * xbuf_ref[HALO - j:HALO - j + tt, :]
    xbuf_ref[0:HALO, :] = xbuf_ref[tt:tt + HALO, :]

    xcb = xc.astype(BF16)
    half = LRU_DIM // 2
    ra = jnp.concatenate([_dot(xcb[:, p * half:(p + 1) * half], wa_ref[p]) for p in range(2)], axis=1)
    ia = jnp.concatenate([_dot(xcb[:, p * half:(p + 1) * half], wx_ref[p]) for p in range(2)], axis=1)
    r = jax.nn.sigmoid(ra + ba_ref[...])
    i = jax.nn.sigmoid(ia + bx_ref[...])
    log_a = -LRU_C * r * _softplus(-lam_ref[...])
    a = jnp.exp(log_a)
    u = jnp.sqrt(1.0 - a * a) * (i * xc)

    row = lax.broadcasted_iota(jnp.int32, (tt, LRU_DIM), 0)
    d = 1
    while d < tt:
        u = a * _shift_rows(u, d, 0.0, row) + u
        a = a * _shift_rows(a, d, 1.0, row)
        d *= 2
    h = u + a * h_ref[...]
    h_ref[...] = h[tt - 1:tt, :]
    o_ref[...] = (h * jax.nn.gelu(yl)).astype(o_ref.dtype)


def _lru_mixer(z, conv_w, conv_b, wa_bd, b_a, wx_bd, b_x, lam, *, tt=256):
    b, t, _ = z.shape
    vec = lambda a: a.reshape(1, LRU_DIM)
    const2 = lambda shape: pl.BlockSpec(shape, lambda i, j: (0, 0))
    const3 = lambda shape: pl.BlockSpec(shape, lambda i, j: (0, 0, 0))
    half = LRU_DIM // 2
    return pl.pallas_call(
        functools.partial(_lru_kernel, tt=tt),
        grid=(b, t // tt),
        in_specs=[pl.BlockSpec((None, tt, LRU_IN), lambda i, j: (i, j, 0)),
                  const2((LRU_CONV, LRU_DIM)), const2((1, LRU_DIM)),
                  const3((2, half, half)), const2((1, LRU_DIM)),
                  const3((2, half, half)), const2((1, LRU_DIM)), const2((1, LRU_DIM))],
        out_specs=pl.BlockSpec((None, tt, LRU_DIM), lambda i, j: (i, j, 0)),
        out_shape=jax.ShapeDtypeStruct((b, t, LRU_DIM), BF16),
        scratch_shapes=[pltpu.VMEM((tt + HALO, LRU_DIM), F32), pltpu.VMEM((1, LRU_DIM), F32)],
        compiler_params=_params(("parallel", "arbitrary")),
        name="rglru",
    )(z, conv_w, vec(conv_b), wa_bd, vec(b_a), wx_bd, vec(b_x), vec(lam))


def _gdn_kernel(z_ref, cw_ref, alog_ref, dtb_ref, ng_ref, o_ref,
                xbuf_ref, q_ref, k_ref, v_ref, beta_ref, g_ref, s_ref, *, tt):
    c = CHUNK
    hd = GDN_HEAD_DIM
    qkv_dim = 3 * GDN_DIM

    @pl.when(pl.program_id(1) == 0)
    def _():
        xbuf_ref[0:HALO, :] = jnp.zeros((HALO, qkv_dim), F32)
        s_ref[...] = jnp.zeros_like(s_ref)

    xbuf_ref[HALO:HALO + tt, :] = z_ref[:, 0:qkv_dim]
    for s in range(3 * GDN_HEADS):
        sl = slice(s * hd, (s + 1) * hd)
        acc = cw_ref[GDN_CONV - 1:GDN_CONV, sl] * xbuf_ref[HALO:HALO + tt, sl]
        for j in range(1, GDN_CONV):
            acc = acc + cw_ref[GDN_CONV - 1 - j:GDN_CONV - j, sl] * xbuf_ref[HALO - j:HALO - j + tt, sl]
        y = _silu(acc)
        which, head = divmod(s, GDN_HEADS)
        if which < 2:
            y = y * lax.rsqrt(jnp.sum(y * y, axis=-1, keepdims=True) + 1e-12)
        if which == 0:
            y = y * (hd ** -0.5)
        (q_ref, k_ref, v_ref)[which][:, head * hd:(head + 1) * hd] = y
    xbuf_ref[0:HALO, :] = xbuf_ref[tt:tt + HALO, :]

    small = z_ref[:, 4 * GDN_DIM:4 * GDN_DIM + GDN_SMALL]
    beta_ref[...] = jax.nn.sigmoid(small)
    g_ref[...] = -jnp.exp(alog_ref[...]) * _softplus(small + dtb_ref[...])

    m = _pair_masks()
    row_s = lax.broadcasted_iota(jnp.int32, (c, GDN_SMALL), 0)
    ng = ng_ref[...]

    def chunk(ci, carry):
        rows = pl.ds(pl.multiple_of(ci * c, c), c)
        gc = _cumsum_rows(g_ref[rows, :], row_s)
        gct = gc.T
        beta = beta_ref[rows, :]
        heads = range(GDN_HEADS)
        pairs = range(GDN_HEADS // 2)
        sls = [slice(h * hd, (h + 1) * hd) for h in heads]
        sel = [m["first"], jnp.logical_not(m["first"])]
        gcol = [gc[:, GDN_HEADS + h:GDN_HEADS + h + 1] for h in heads]
        grow = [gct[GDN_HEADS + h:GDN_HEADS + h + 1, :] for h in heads]
        bcol = [beta[:, h:h + 1] for h in heads]
        q = [q_ref[rows, sl] for sl in sls]
        k = [k_ref[rows, sl] for sl in sls]
        v = [v_ref[rows, sl] for sl in sls]
        kb = [k[h] * bcol[h] for h in heads]
        egc = [jnp.exp(gcol[h]) for h in heads]
        zero = jnp.zeros((c, hd), BF16)
        kbf = [x.astype(BF16) for x in k]
        mm = []
        qk = []
        for p in pairs:
            h0, h1 = 2 * p, 2 * p + 1
            gcol2 = jnp.where(m["first"], gcol[h0], gcol[h1])
            grow2 = jnp.concatenate([grow[h0], grow[h1]], axis=1)
            decay = jnp.where(m["incl"], jnp.exp(jnp.where(m["incl"], gcol2 - grow2, 0.0)), 0.0)
            lhs = jnp.concatenate([jnp.concatenate([kb[h0], kb[h1]], axis=1),
                                   jnp.concatenate([q[h0], q[h1]], axis=1)], axis=0).astype(BF16)
            rhs = jnp.concatenate([jnp.concatenate([kbf[h0], zero], axis=1),
                                   jnp.concatenate([zero, kbf[h1]], axis=1)], axis=0)
            kq = _dot_nt(lhs, rhs)
            mm.append(jnp.where(m["strict"], -(kq[0:c, :] * decay), 0.0))
            qk.append((kq[c:2 * c, :] * decay))
        t = _pair_inverse_many(mm, m["eye"], m["bd"])
        rhs_sol = [jnp.concatenate(
            [jnp.concatenate([v[h] * bcol[h], kb[h] * egc[h]], axis=1) for h in (2 * p, 2 * p + 1)],
            axis=0).astype(BF16) for p in pairs]
        sol = [_dot(jnp.where(sel[h % 2], t[h // 2], 0.0).astype(BF16), rhs_sol[h // 2])
               for h in heads]
        qe = [(q[h] * egc[h]).astype(BF16) for h in heads]
        g_last = [gcol[h][c - 1:c, :] for h in heads]
        kd = [(k[h] * jnp.exp(g_last[h] - gcol[h])).T.astype(BF16) for h in heads]
        s_mat = [s_ref[h] for h in heads]
        s_b = [x.astype(BF16) for x in s_mat]
        vn = [sol[h][:, 0:hd] - _dot(sol[h][:, hd:2 * hd].astype(BF16), s_b[h]) for h in heads]
        vnb = [x.astype(BF16) for x in vn]
        for h in heads:
            s_ref[h] = s_mat[h] * jnp.exp(g_last[h]) + _dot(kd[h], vnb[h])
        for h in heads:
            p = h // 2
            vn2 = jnp.concatenate([vnb[2 * p], vnb[2 * p + 1]], axis=0)
            oo = _dot(qe[h], s_b[h]) + _dot(jnp.where(sel[h % 2], qk[p], 0.0).astype(BF16), vn2)
            gate = z_ref[rows, 3 * GDN_DIM + h * hd:3 * GDN_DIM + (h + 1) * hd]
            o_ref[rows, sls[h]] = (_rms(oo, ng) * _silu(gate)).astype(o_ref.dtype)
        return carry

    lax.fori_loop(0, tt // c, chunk, 0)


def _gdn_mixer(z, conv_w, a_log_pad, dt_bias_pad, norm_g, *, tt=256):
    b, t, _ = z.shape
    const2 = lambda shape: pl.BlockSpec(shape, lambda i, j: (0, 0))
    return pl.pallas_call(
        functools.partial(_gdn_kernel, tt=tt),
        grid=(b, t // tt),
        in_specs=[pl.BlockSpec((None, tt, GDN_Z), lambda i, j: (i, j, 0)),
                  const2((GDN_CONV, 3 * GDN_DIM)), const2((1, GDN_SMALL)), const2((1, GDN_SMALL)),
                  const2((1, GDN_HEAD_DIM))],
        out_specs=pl.BlockSpec((None, tt, GDN_DIM), lambda i, j: (i, j, 0)),
        out_shape=jax.ShapeDtypeStruct((b, t, GDN_DIM), BF16),
        scratch_shapes=[pltpu.VMEM((tt + HALO, 3 * GDN_DIM), F32),
                        pltpu.VMEM((tt, GDN_DIM), F32), pltpu.VMEM((tt, GDN_DIM), F32),
                        pltpu.VMEM((tt, GDN_DIM), F32),
                        pltpu.VMEM((tt, GDN_SMALL), F32), pltpu.VMEM((tt, GDN_SMALL), F32),
                        pltpu.VMEM((GDN_HEADS, GDN_HEAD_DIM, GDN_HEAD_DIM), F32)],
        compiler_params=_params(("parallel", "arbitrary")),
        name="gdn",
    )(z, conv_w, a_log_pad, dt_bias_pad, norm_g.reshape(1, GDN_HEAD_DIM))


def _rwkv_kernel(z_ref, mu_ref, w0_ref, w2_ref, a0_ref, a2_ref, g2_ref, kk_ref, ka_ref, rk_ref,
                 lng_ref, lnb_ref, o_ref,
                 zbuf_ref, r_ref, k_ref, v_ref, n_ref, a_ref, lw_ref, y_ref, p_ref, *, tt):
    c = CHUNK
    w = V7X_LANES
    n_slabs = RWKV_DIM // w

    @pl.when(pl.program_id(1) == 0)
    def _():
        zbuf_ref[0:HALO, :] = jnp.zeros((HALO, RWKV_IN), F32)
        p_ref[...] = jnp.zeros_like(p_ref)

    m = _pair_masks()
    ones_bd = jnp.where(m["bd"], 1.0, 0.0).astype(BF16)
    not_first = jnp.logical_not(m["first"])

    zbuf_ref[HALO:HALO + tt, :] = z_ref[...]

    def shifted(lo, hi):
        cur = zbuf_ref[HALO:HALO + tt, lo:hi]
        prev = zbuf_ref[HALO - 1:HALO - 1 + tt, lo:hi]
        return cur + mu_ref[:, lo:hi] * (prev - cur)

    lora = shifted(3 * RWKV_DIM, 3 * RWKV_DIM + DECAY_LORA + ICLR_LORA)
    lora_t = jnp.tanh(lora).astype(BF16)
    lora_b = lora.astype(BF16)
    gate_in = jax.nn.sigmoid(shifted(RWKV_IN - GATE_LORA, RWKV_IN)).astype(BF16)
    for s in range(n_slabs):
        sl = slice(s * w, (s + 1) * w)
        r = shifted(s * w, (s + 1) * w)
        k = shifted(RWKV_DIM + s * w, RWKV_DIM + (s + 1) * w)
        v = shifted(2 * RWKV_DIM + s * w, 2 * RWKV_DIM + (s + 1) * w)
        w_log = -_softplus(-(w0_ref[:, sl] + _dot(lora_t, w2_ref[:, sl]))) - 0.5
        a = jax.nn.sigmoid(a0_ref[:, sl] + _dot(lora_b, a2_ref[:, sl]))
        kn = k * kk_ref[:, sl]
        kn = kn * lax.rsqrt(_seg_sum(kn * kn, ones_bd) + 1e-12)
        r_ref[:, sl] = r
        k_ref[:, sl] = k * (1.0 + (a - 1.0) * ka_ref[:, sl])
        v_ref[:, sl] = v
        n_ref[:, sl] = kn
        a_ref[:, sl] = a
        lw_ref[:, sl] = -jnp.exp(w_log)
    zbuf_ref[0:HALO, :] = zbuf_ref[tt:tt + HALO, :]

    def chunk(ci, carry):
        rows = pl.ds(pl.multiple_of(ci * c, c), c)
        slabs = range(n_slabs)
        sls = [slice(s * w, (s + 1) * w) for s in slabs]
        bd = lambda x: _block_diag(x, m["bd"]).astype(BF16)
        r = [r_ref[rows, sl] for sl in sls]
        k = [k_ref[rows, sl] for sl in sls]
        v = [v_ref[rows, sl] for sl in sls]
        kn = [n_ref[rows, sl] for sl in sls]
        a = [a_ref[rows, sl] for sl in sls]
        lw = [lw_ref[rows, sl] for sl in sls]
        cum = [_cumsum_rows(x, m["row"]) for x in lw]
        b = [kn[s] * a[s] for s in slabs]
        at = [-kn[s] * jnp.exp(cum[s] - lw[s]) for s in slabs]
        rt = [(r[s] * jnp.exp(cum[s])).astype(BF16) for s in slabs]
        e_out = [jnp.exp(-x) for x in cum]
        bt = [b[s] * e_out[s] for s in slabs]
        kt = [k[s] * e_out[s] for s in slabs]
        aa = []
        for s in slabs:
            lhs = jnp.concatenate([at[s].astype(BF16), rt[s]], axis=0)
            rhs = jnp.concatenate(
                [jnp.where(m["first"], bt[s], 0.0), jnp.where(not_first, bt[s], 0.0),
                 jnp.where(m["first"], kt[s], 0.0), jnp.where(not_first, kt[s], 0.0)],
                axis=0).astype(BF16)
            aa.append(_dot_nt(lhs, rhs))
        a_ab = [jnp.where(m["strict"], x[0:c, 0:w], 0.0) for x in aa]
        a_ak = [jnp.where(m["strict"], x[0:c, w:2 * w], 0.0).astype(BF16) for x in aa]
        a_rb = [jnp.where(m["incl"], x[c:2 * c, 0:w], 0.0).astype(BF16) for x in aa]
        a_rk = [jnp.where(m["incl"], x[c:2 * c, w:2 * w], 0.0).astype(BF16) for x in aa]
        v_bd = [bd(x) for x in v]
        akv = [_dot(a_ak[s], v_bd[s]) for s in slabs]
        y0 = [_dot(a_rk[s], v_bd[s]) for s in slabs]
        t = _pair_inverse_many(a_ab, m["eye"], m["bd"])
        tu = [_dot(t[s].astype(BF16), jnp.concatenate([bd(akv[s]), bd(at[s])], axis=1))
              for s in slabs]
        kb_t = []
        d_col = []
        for s in slabs:
            d_end = jnp.exp(cum[s][c - 1:c, :] - cum[s])
            kb_t.append(jnp.concatenate([k[s] * d_end, b[s] * d_end], axis=0).T.astype(BF16))
            d_col.append(jnp.exp(cum[s].T[:, c - 1:c]))
        p_mat = [p_ref[s] for s in slabs]
        p_b = [x.astype(BF16) for x in p_mat]
        u = [tu[s][:, 0:w] + _dot(tu[s][:, w:2 * w].astype(BF16), p_b[s]) for s in slabs]
        upd = [_dot(kb_t[s], jnp.concatenate([v[s], u[s]], axis=0).astype(BF16)) for s in slabs]
        for s in slabs:
            p_ref[s] = d_col[s] * p_mat[s] + jnp.where(m["bd"], upd[s], 0.0)
        for s in slabs:
            y_ref[rows, sls[s]] = y0[s] + _dot(rt[s], p_b[s]) + _dot(a_rb[s], bd(u[s]))
        return carry

    lax.fori_loop(0, tt // c, chunk, 0)

    inv_n = 1.0 / RWKV_HEAD_DIM
    for s in range(n_slabs):
        sl = slice(s * w, (s + 1) * w)
        y = y_ref[:, sl]
        mean = _seg_sum(y, ones_bd) * inv_n
        yc = y - mean
        var = _seg_sum(yc * yc, ones_bd) * inv_n
        y = yc * lax.rsqrt(var + RWKV_GN_EPS) * lng_ref[:, sl] + lnb_ref[:, sl]
        bonus = _seg_sum(r_ref[:, sl] * k_ref[:, sl] * rk_ref[:, sl], ones_bd) * v_ref[:, sl]
        g = _dot(gate_in, g2_ref[:, sl])
        o_ref[:, sl] = ((y + bonus) * g).astype(o_ref.dtype)


def _rwkv_mixer(z, mu, w0, w2_pad, a0, a2_pad, g2, k_k, k_a, r_k, ln_g, ln_b, *, tt=256):
    b, t, _ = z.shape
    lora = DECAY_LORA + ICLR_LORA
    const2 = lambda shape: pl.BlockSpec(shape, lambda i, j: (0, 0))
    vec = lambda a: a.reshape(1, RWKV_DIM)
    tile = pltpu.VMEM((tt, RWKV_DIM), F32)
    return pl.pallas_call(
        functools.partial(_rwkv_kernel, tt=tt),
        grid=(b, t // tt),
        in_specs=[pl.BlockSpec((None, tt, RWKV_IN), lambda i, j: (i, j, 0)),
                  const2((1, RWKV_IN)), const2((1, RWKV_DIM)), const2((lora, RWKV_DIM)),
                  const2((1, RWKV_DIM)), const2((lora, RWKV_DIM)), const2((GATE_LORA, RWKV_DIM))]
                 + [const2((1, RWKV_DIM))] * 5,
        out_specs=pl.BlockSpec((None, tt, RWKV_DIM), lambda i, j: (i, j, 0)),
        out_shape=jax.ShapeDtypeStruct((b, t, RWKV_DIM), BF16),
        scratch_shapes=[pltpu.VMEM((tt + HALO, RWKV_IN), F32)] + [tile] * 7
                       + [pltpu.VMEM((RWKV_DIM // V7X_LANES, V7X_LANES, V7X_LANES), F32)],
        compiler_params=_params(("parallel", "arbitrary")),
        name="rwkv7",
    )(z, mu.reshape(1, RWKV_IN), vec(w0), w2_pad, vec(a0), a2_pad, g2, vec(k_k), vec(k_a),
      vec(r_k), vec(ln_g), vec(ln_b))


def _block_diag_weights(w):
    per = (LRU_DIM // 2) // LRU_BLOCK_DIM
    out = jnp.zeros((2, LRU_DIM // 2, LRU_DIM // 2), F32)
    for blk in range(LRU_BLOCKS):
        p, i = divmod(blk, per)
        lo = i * LRU_BLOCK_DIM
        out = out.at[p, lo:lo + LRU_BLOCK_DIM, lo:lo + LRU_BLOCK_DIM].set(w[blk])
    return out.astype(BF16)


def kernel(x, norm1_g, ffn1_wg, ffn1_wu, ffn1_wd, norm_mix_g, w_in, rw_mu, rw_w0, rw_w2, rw_a0, rw_a2, rw_g2, rw_kk, rw_ka, rw_rk, rw_ln_g, rw_ln_b, gd_conv_w, gd_a_log, gd_dt_bias, gd_norm_g, lr_conv_w, lr_conv_b, lr_wa, lr_ba, lr_wx, lr_bx, lr_lam, w_out, norm2_g, ffn2_wg, ffn2_wu, ffn2_wd, final_g):
    b, t, d = x.shape
    n = b * t
    bf = lambda a: a.astype(BF16)
    xs = x.reshape(n, d)
    for l in range(DEPTH):
        xs = _ffn(xs, norm1_g[l], bf(ffn1_wg[l]), bf(ffn1_wu[l]), bf(ffn1_wd[l]))

        w_l = w_in[l]
        w_rw = bf(w_l[:, :RWKV_IN])
        w_gd = bf(jnp.pad(w_l[:, RWKV_IN:RWKV_IN + GDN_IN], ((0, 0), (0, GDN_Z - GDN_IN))))
        w_lr = bf(w_l[:, RWKV_IN + GDN_IN:])
        z_rw = _norm_proj(xs, norm_mix_g[l], w_rw, name="in_proj_rwkv").reshape(b, t, RWKV_IN)
        z_gd = _norm_proj(xs, norm_mix_g[l], w_gd, name="in_proj_gdn").reshape(b, t, GDN_Z)
        z_lr = _norm_proj(xs, norm_mix_g[l], w_lr, name="in_proj_lru").reshape(b, t, LRU_IN)

        w2_pad = bf(jnp.concatenate([rw_w2[l], jnp.zeros((ICLR_LORA, RWKV_DIM), F32)], axis=0))
        a2_pad = bf(jnp.concatenate([jnp.zeros((DECAY_LORA, RWKV_DIM), F32), rw_a2[l]], axis=0))
        y_rw = _rwkv_mixer(z_rw, rw_mu[l], rw_w0[l], w2_pad, rw_a0[l], a2_pad, bf(rw_g2[l]),
                           rw_kk[l], rw_ka[l], rw_rk[l].reshape(RWKV_DIM), rw_ln_g[l], rw_ln_b[l])

        lane_pad = lambda a: jnp.pad(a, (GDN_HEADS, GDN_SMALL - 2 * GDN_HEADS)).reshape(1, GDN_SMALL)
        y_gd = _gdn_mixer(z_gd, gd_conv_w[l], lane_pad(gd_a_log[l]), lane_pad(gd_dt_bias[l]),
                          gd_norm_g[l])

        y_lr = _lru_mixer(z_lr, lr_conv_w[l], lr_conv_b[l], _block_diag_weights(lr_wa[l]), lr_ba[l],
                          _block_diag_weights(lr_wx[l]), lr_bx[l], lr_lam[l])

        w_o = bf(w_out[l])
        xs = _out_proj(xs, y_rw.reshape(n, RWKV_DIM), y_gd.reshape(n, GDN_DIM),
                       y_lr.reshape(n, LRU_DIM), w_o[:RWKV_DIM], w_o[RWKV_DIM:RWKV_DIM + GDN_DIM],
                       w_o[RWKV_DIM + GDN_DIM:])
        last = l == DEPTH - 1
        xs = _ffn(xs, norm2_g[l], bf(ffn2_wg[l]), bf(ffn2_wu[l]), bf(ffn2_wd[l]),
                  final_g if last else None)
    return xs.reshape(b, t, d)
```

```python
import functools
import math

import jax
import jax.numpy as jnp
from jax import lax
from jax.experimental import pallas as pl
from jax.experimental.pallas import tpu as pltpu

F32 = jnp.float32
BF16 = jnp.bfloat16

D_MODEL = 2048
DEPTH = 2
CHUNK = 64
RMS_EPS = 1e-6
D_FF = 5632
RWKV_HEADS = 12
RWKV_HEAD_DIM = 64
RWKV_DIM = RWKV_HEADS * RWKV_HEAD_DIM
DECAY_LORA = 64
ICLR_LORA = 64
GATE_LORA = 128
RWKV_GN_EPS = 64e-5
RWKV_DECAY_SCALE = math.exp(-0.5)
GDN_HEADS = 6
GDN_HEAD_DIM = 128
GDN_DIM = GDN_HEADS * GDN_HEAD_DIM
GDN_CONV = 4
LRU_BLOCKS = 8
LRU_BLOCK_DIM = 64
LRU_DIM = LRU_BLOCKS * LRU_BLOCK_DIM
LRU_CONV = 4
LRU_C = 8.0
RWKV_IN = 3 * RWKV_DIM + DECAY_LORA + ICLR_LORA + GATE_LORA
GDN_IN = 4 * GDN_DIM + 2 * GDN_HEADS
LRU_IN = 2 * LRU_DIM

V7X_LANES = 128
V7X_SUBLANES = 8
V7X_VMEM_BYTES = 64 * 1024 * 1024
VMEM_LIMIT_BYTES = V7X_VMEM_BYTES - 8 * 1024 * 1024

GDN_SMALL = V7X_LANES
GDN_Z = 4 * GDN_DIM + GDN_SMALL
HALO = V7X_SUBLANES


def _params(sem):
    return pltpu.CompilerParams(dimension_semantics=sem, vmem_limit_bytes=VMEM_LIMIT_BYTES)


def _dot(a, b, precision=None):
    return jnp.dot(a, b, preferred_element_type=F32, precision=precision)


def _dot_nt(a, b):
    return lax.dot_general(a, b, (((1,), (1,)), ((), ())), preferred_element_type=F32)


def _silu(x):
    return x * jax.nn.sigmoid(x)


def _softplus(x):
    return jnp.maximum(x, 0.0) + jnp.log1p(jnp.exp(-jnp.abs(x)))


def _rms(x, g):
    return x * lax.rsqrt(jnp.mean(x * x, axis=-1, keepdims=True) + RMS_EPS) * g


def _ffn_kernel(*refs, nf, final):
    if final:
        x_ref, g_ref, wg_ref, wu_ref, wd_ref, fg_ref, o_ref, h_ref = refs
    else:
        x_ref, g_ref, wg_ref, wu_ref, wd_ref, o_ref, h_ref = refs
    j = pl.program_id(1)

    @pl.when(j == 0)
    def _():
        h_ref[...] = _rms(x_ref[...], g_ref[...]).astype(BF16)
        o_ref[...] = jnp.zeros_like(o_ref)

    h = h_ref[...]
    a = _dot(h, wg_ref[...])
    u = _dot(h, wu_ref[...])
    act = (_silu(a) * u).astype(BF16)
    o_ref[...] += _dot(act, wd_ref[...])

    @pl.when(j == nf - 1)
    def _():
        y = x_ref[...] + 0.5 * o_ref[...]
        if final:
            y = _rms(y, fg_ref[...])
        o_ref[...] = y


def _ffn(x, g, wg, wu, wd, final_g=None, *, tm=512, tf=512):
    n, d = x.shape
    f = wg.shape[1]
    final = final_g is not None
    row = lambda i, j: (0, 0)
    in_specs = [
        pl.BlockSpec((tm, d), lambda i, j: (i, 0)),
        pl.BlockSpec((1, d), row),
        pl.BlockSpec((d, tf), lambda i, j: (0, j)),
        pl.BlockSpec((d, tf), lambda i, j: (0, j)),
        pl.BlockSpec((tf, d), lambda i, j: (j, 0)),
    ]
    args = [x, g.reshape(1, d), wg, wu, wd]
    if final:
        in_specs.append(pl.BlockSpec((1, d), row))
        args.append(final_g.reshape(1, d))
    return pl.pallas_call(
        functools.partial(_ffn_kernel, nf=f // tf, final=final),
        grid=(n // tm, f // tf),
        in_specs=in_specs,
        out_specs=pl.BlockSpec((tm, d), lambda i, j: (i, 0)),
        out_shape=jax.ShapeDtypeStruct((n, d), F32),
        scratch_shapes=[pltpu.VMEM((tm, d), BF16)],
        compiler_params=_params(("parallel", "arbitrary")),
        name="ffn",
    )(*args)


def _norm_proj_kernel(x_ref, g_ref, w_ref, o_ref):
    h = _rms(x_ref[...], g_ref[...]).astype(BF16)
    o_ref[...] = _dot(h, w_ref[...])


def _norm_proj(x, g, w, *, tm=512, name):
    n, d = x.shape
    c = w.shape[1]
    return pl.pallas_call(
        _norm_proj_kernel,
        grid=(n // tm,),
        in_specs=[
            pl.BlockSpec((tm, d), lambda i: (i, 0)),
            pl.BlockSpec((1, d), lambda i: (0, 0)),
            pl.BlockSpec((d, c), lambda i: (0, 0)),
        ],
        out_specs=pl.BlockSpec((tm, c), lambda i: (i, 0)),
        out_shape=jax.ShapeDtypeStruct((n, c), F32),
        compiler_params=_params(("parallel",)),
        name=name,
    )(x, g.reshape(1, d), w)


def _out_proj_kernel(x_ref, yr_ref, yg_ref, yl_ref, wr_ref, wg_ref, wl_ref, o_ref):
    o_ref[...] = (x_ref[...] + _dot(yr_ref[...], wr_ref[...]) + _dot(yg_ref[...], wg_ref[...])
                  + _dot(yl_ref[...], wl_ref[...]))


def _out_proj(x, y_rw, y_gd, y_lr, w_rw, w_gd, w_lr, *, tm=512):
    n, d = x.shape
    tile = lambda c: pl.BlockSpec((tm, c), lambda i: (i, 0))
    whole = lambda c: pl.BlockSpec((c, d), lambda i: (0, 0))
    return pl.pallas_call(
        _out_proj_kernel,
        grid=(n // tm,),
        in_specs=[tile(d), tile(RWKV_DIM), tile(GDN_DIM), tile(LRU_DIM),
                  whole(RWKV_DIM), whole(GDN_DIM), whole(LRU_DIM)],
        out_specs=tile(d),
        out_shape=jax.ShapeDtypeStruct((n, d), F32),
        compiler_params=_params(("parallel",)),
        name="out_proj",
    )(x, y_rw, y_gd, y_lr, w_rw, w_gd, w_lr)


def _shift_rows(x, d, fill, row):
    return jnp.where(row >= d, pltpu.roll(x, d, 0), fill)


def _cumsum_rows(x, row):
    d = 1
    while d < x.shape[0]:
        x = x + _shift_rows(x, d, 0.0, row)
        d *= 2
    return x


def _block_diag(m, bd_mask):
    mb = m.astype(BF16)
    return jnp.where(bd_mask, jnp.concatenate([mb, mb], axis=0), jnp.zeros((), BF16))


def _pair_inverse_many(xs, eye2, bd_mask):
    c = CHUNK
    bd = lambda x: _block_diag(x, bd_mask)
    ss = [eye2 + x for x in xs]
    ps = [_dot(x.astype(BF16), bd(x)) for x in xs]
    for _ in range(c.bit_length() - 3):
        prods = [_dot(jnp.concatenate([p, s], axis=0).astype(BF16), bd(p)) for p, s in zip(ps, ss)]
        ps = [x[0:c] for x in prods]
        ss = [s + x[c:2 * c] for s, x in zip(ss, prods)]
    return [s + _dot(s.astype(BF16), bd(p)) for p, s in zip(ps, ss)]


def _seg_sum(x, ones_bd):
    hi = x.astype(BF16)
    lo = (x - hi.astype(F32)).astype(BF16)
    return _dot(hi, ones_bd) + _dot(lo, ones_bd)


def _pair_masks():
    c = CHUNK
    row = lax.broadcasted_iota(jnp.int32, (c, 2 * c), 0)
    lane = lax.broadcasted_iota(jnp.int32, (c, 2 * c), 1)
    col = jnp.where(lane >= c, lane - c, lane)
    r2 = lax.broadcasted_iota(jnp.int32, (2 * c, 2 * c), 0)
    l2 = lax.broadcasted_iota(jnp.int32, (2 * c, 2 * c), 1)
    bd_mask = (r2 >= c) == (l2 >= c)
    return dict(row=row, first=lane < c, strict=row > col, incl=row >= col,
                eye=(row == col).astype(F32), bd=bd_mask)


def _lru_kernel(z_ref, cw_ref, cb_ref, wa_ref, ba_ref, wx_ref, bx_ref, lam_ref, o_ref,
                xbuf_ref, h_ref, *, tt):
    @pl.when(pl.program_id(1) == 0)
    def _():
        xbuf_ref[0:HALO, :] = jnp.zeros((HALO, LRU_DIM), F32)
        h_ref[...] = jnp.zeros_like(h_ref)

    xl = z_ref[:, 0:LRU_DIM]
    yl = z_ref[:, LRU_DIM:2 * LRU_DIM]
    xbuf_ref[HALO:HALO + tt, :] = xl
    xc = cb_ref[...] + cw_ref[LRU_CONV - 1:LRU_CONV, :] * xl
    for j in range(1, LRU_CONV):
        xc = xc + cw_ref[LRU_CONV - 1 - j:LRU_CONV - j, :] * xbuf_ref[HALO - j:HALO - j + tt, :]
    xbuf_ref[0:HALO, :] = xbuf_ref[tt:tt + HALO, :]

    xcb = xc.astype(BF16)
    half = LRU_DIM // 2
    ra = jnp.concatenate([_dot(xcb[:, p * half:(p + 1) * half], wa_ref[p]) for p in range(2)], axis=1)
    ia = jnp.concatenate([_dot(xcb[:, p * half:(p + 1) * half], wx_ref[p]) for p in range(2)], axis=1)
    r = jax.nn.sigmoid(ra + ba_ref[...])
    i = jax.nn.sigmoid(ia + bx_ref[...])
    log_a = -LRU_C * r * _softplus(-lam_ref[...])
    a = jnp.exp(log_a)
    u = jnp.sqrt(1.0 - a * a) * (i * xc)

    row = lax.broadcasted_iota(jnp.int32, (tt, LRU_DIM), 0)
    d = 1
    while d < tt:
        u = a * _shift_rows(u, d, 0.0, row) + u
        a = a * _shift_rows(a, d, 1.0, row)
        d *= 2
    h = u + a * h_ref[...]
    h_ref[...] = h[tt - 1:tt, :]
    o_ref[...] = (h * jax.nn.gelu(yl)).astype(o_ref.dtype)


def _lru_mixer(z, conv_w, conv_b, wa_bd, b_a, wx_bd, b_x, lam, *, tt=256):
    b, t, _ = z.shape
    vec = lambda a: a.reshape(1, LRU_DIM)
    const2 = lambda shape: pl.BlockSpec(shape, lambda i, j: (0, 0))
    const3 = lambda shape: pl.BlockSpec(shape, lambda i, j: (0, 0, 0))
    half = LRU_DIM // 2
    return pl.pallas_call(
        functools.partial(_lru_kernel, tt=tt),
        grid=(b, t // tt),
        in_specs=[pl.BlockSpec((None, tt, LRU_IN), lambda i, j: (i, j, 0)),
                  const2((LRU_CONV, LRU_DIM)), const2((1, LRU_DIM)),
                  const3((2, half, half)), const2((1, LRU_DIM)),
                  const3((2, half, half)), const2((1, LRU_DIM)), const2((1, LRU_DIM))],
        out_specs=pl.BlockSpec((None, tt, LRU_DIM), lambda i, j: (i, j, 0)),
        out_shape=jax.ShapeDtypeStruct((b, t, LRU_DIM), BF16),
        scratch_shapes=[pltpu.VMEM((tt + HALO, LRU_DIM), F32), pltpu.VMEM((1, LRU_DIM), F32)],
        compiler_params=_params(("parallel", "arbitrary")),
        name="rglru",
    )(z, conv_w, vec(conv_b), wa_bd, vec(b_a), wx_bd, vec(b_x), vec(lam))


def _gdn_kernel(z_ref, cw_ref, alog_ref, dtb_ref, ng_ref, o_ref,
                xbuf_ref, q_ref, k_ref, v_ref, beta_ref, g_ref, s_ref, *, tt, group):
    c = CHUNK
    hd = GDN_HEAD_DIM
    qkv_dim = 3 * GDN_DIM

    @pl.when(pl.program_id(1) == 0)
    def _():
        xbuf_ref[0:HALO, :] = jnp.zeros((HALO, qkv_dim), F32)
        s_ref[...] = jnp.zeros_like(s_ref)

    xbuf_ref[HALO:HALO + tt, :] = z_ref[:, 0:qkv_dim]
    for s in range(3 * GDN_HEADS):
        sl = slice(s * hd, (s + 1) * hd)
        acc = cw_ref[GDN_CONV - 1:GDN_CONV, sl] * xbuf_ref[HALO:HALO + tt, sl]
        for j in range(1, GDN_CONV):
            acc = acc + cw_ref[GDN_CONV - 1 - j:GDN_CONV - j, sl] * xbuf_ref[HALO - j:HALO - j + tt, sl]
        y = _silu(acc)
        which, head = divmod(s, GDN_HEADS)
        if which < 2:
            y = y * lax.rsqrt(jnp.sum(y * y, axis=-1, keepdims=True) + 1e-12)
        if which == 0:
            y = y * (hd ** -0.5)
        (q_ref, k_ref, v_ref)[which][:, head * hd:(head + 1) * hd] = y
    xbuf_ref[0:HALO, :] = xbuf_ref[tt:tt + HALO, :]

    small = z_ref[:, 4 * GDN_DIM:4 * GDN_DIM + GDN_SMALL]
    beta_ref[...] = jax.nn.sigmoid(small)
    g_ref[...] = -jnp.exp(alog_ref[...]) * _softplus(small + dtb_ref[...])

    m = _pair_masks()
    row_s = lax.broadcasted_iota(jnp.int32, (c, GDN_SMALL), 0)
    ng = ng_ref[...]

    def chunk_group(row_slices):
        nh = GDN_HEADS
        chunks = range(len(row_slices))
        hp = [(j, h) for j in chunks for h in range(nh)]
        pp = [(j, p) for j in chunks for p in range(nh // 2)]
        sls = [slice(h * hd, (h + 1) * hd) for h in range(nh)]
        sel = [m["first"], jnp.logical_not(m["first"])]
        gc = [_cumsum_rows(g_ref[rows, :], row_s) for rows in row_slices]
        gct = [x.T for x in gc]
        beta = [beta_ref[rows, :] for rows in row_slices]
        gcol = [gc[j][:, nh + h:nh + h + 1] for j, h in hp]
        grow = [gct[j][nh + h:nh + h + 1, :] for j, h in hp]
        bcol = [beta[j][:, h:h + 1] for j, h in hp]
        q = [q_ref[row_slices[j], sls[h]] for j, h in hp]
        k = [k_ref[row_slices[j], sls[h]] for j, h in hp]
        v = [v_ref[row_slices[j], sls[h]] for j, h in hp]
        n = range(len(hp))
        kb = [k[i] * bcol[i] for i in n]
        egc = [jnp.exp(gcol[i]) for i in n]
        zero = jnp.zeros((c, hd), BF16)
        kbf = [x.astype(BF16) for x in k]
        mm = []
        qk = []
        for j, p in pp:
            i0, i1 = j * nh + 2 * p, j * nh + 2 * p + 1
            gcol2 = jnp.where(m["first"], gcol[i0], gcol[i1])
            grow2 = jnp.concatenate([grow[i0], grow[i1]], axis=1)
            decay = jnp.where(m["incl"], jnp.exp(jnp.where(m["incl"], gcol2 - grow2, 0.0)), 0.0)
            lhs = jnp.concatenate([jnp.concatenate([kb[i0], kb[i1]], axis=1),
                                   jnp.concatenate([q[i0], q[i1]], axis=1)], axis=0).astype(BF16)
            rhs = jnp.concatenate([jnp.concatenate([kbf[i0], zero], axis=1),
                                   jnp.concatenate([zero, kbf[i1]], axis=1)], axis=0)
            kq = _dot_nt(lhs, rhs)
            mm.append(jnp.where(m["strict"], -(kq[0:c, :] * decay), 0.0))
            qk.append((kq[c:2 * c, :] * decay))
        qk2 = [jnp.concatenate([jnp.where(sel[0], x, 0.0), jnp.where(sel[1], x, 0.0)], axis=0).astype(BF16)
               for x in qk]
        t = _pair_inverse_many(mm, m["eye"], m["bd"])
        rhs_sol = [jnp.concatenate(
            [jnp.concatenate([v[i] * bcol[i], kb[i] * egc[i]], axis=1)
             for i in (j * nh + 2 * p, j * nh + 2 * p + 1)], axis=0).astype(BF16)
            for j, p in pp]
        sol = [_dot(jnp.where(sel[i % 2], t[i // 2], 0.0).astype(BF16), rhs_sol[i // 2])
               for i in n]
        qe = [(q[i] * egc[i]).astype(BF16) for i in n]
        g_last = [gcol[i][c - 1:c, :] for i in n]
        kd = [(k[i] * jnp.exp(g_last[i] - gcol[i])).T.astype(BF16) for i in n]
        s_mat = [s_ref[h] for h in range(nh)]
        for j in chunks:
            ids = [j * nh + h for h in range(nh)]
            s_b = [x.astype(BF16) for x in s_mat]
            ws = [_dot(jnp.concatenate([sol[i][:, hd:2 * hd].astype(BF16), qe[i]], axis=0), s_b[h])
                  for h, i in enumerate(ids)]
            vnb = [(sol[i][:, 0:hd] - ws[h][0:c]).astype(BF16) for h, i in enumerate(ids)]
            s_mat = [s_mat[h] * jnp.exp(g_last[i]) + _dot(kd[i], vnb[h]) for h, i in enumerate(ids)]
            qv = [_dot(qk2[j * (nh // 2) + p], jnp.concatenate([vnb[2 * p], vnb[2 * p + 1]], axis=0))
                  for p in range(nh // 2)]
            for h, i in enumerate(ids):
                oo = ws[h][c:2 * c] + qv[h // 2][(h % 2) * c:(h % 2 + 1) * c]
                gate = z_ref[row_slices[j], 3 * GDN_DIM + h * hd:3 * GDN_DIM + (h + 1) * hd]
                o_ref[row_slices[j], sls[h]] = (_rms(oo, ng) * _silu(gate)).astype(o_ref.dtype)
        for h in range(nh):
            s_ref[h] = s_mat[h]

    n_chunks = tt // c
    if group == n_chunks:
        chunk_group([slice(j * c, (j + 1) * c) for j in range(n_chunks)])
    else:
        def body(gi, carry):
            base = gi * (group * c)
            chunk_group([pl.ds(pl.multiple_of(base + j * c, c), c) for j in range(group)])
            return carry
        lax.fori_loop(0, n_chunks // group, body, 0)


def _gdn_mixer(z, conv_w, a_log_pad, dt_bias_pad, norm_g, *, tt=512, group=8):
    b, t, _ = z.shape
    const2 = lambda shape: pl.BlockSpec(shape, lambda i, j: (0, 0))
    return pl.pallas_call(
        functools.partial(_gdn_kernel, tt=tt, group=group),
        grid=(b, t // tt),
        in_specs=[pl.BlockSpec((None, tt, GDN_Z), lambda i, j: (i, j, 0)),
                  const2((GDN_CONV, 3 * GDN_DIM)), const2((1, GDN_SMALL)), const2((1, GDN_SMALL)),
                  const2((1, GDN_HEAD_DIM))],
        out_specs=pl.BlockSpec((None, tt, GDN_DIM), lambda i, j: (i, j, 0)),
        out_shape=jax.ShapeDtypeStruct((b, t, GDN_DIM), BF16),
        scratch_shapes=[pltpu.VMEM((tt + HALO, 3 * GDN_DIM), F32),
                        pltpu.VMEM((tt, GDN_DIM), F32), pltpu.VMEM((tt, GDN_DIM), F32),
                        pltpu.VMEM((tt, GDN_DIM), F32),
                        pltpu.VMEM((tt, GDN_SMALL), F32), pltpu.VMEM((tt, GDN_SMALL), F32),
                        pltpu.VMEM((GDN_HEADS, GDN_HEAD_DIM, GDN_HEAD_DIM), F32)],
        compiler_params=_params(("parallel", "arbitrary")),
        name="gdn",
    )(z, conv_w, a_log_pad, dt_bias_pad, norm_g.reshape(1, GDN_HEAD_DIM))


def _rwkv_kernel(z_ref, mu_ref, w0_ref, w2_ref, a0_ref, a2_ref, g2_ref, kk_ref, ka_ref, rk_ref,
                 lng_ref, lnb_ref, o_ref,
                 zbuf_ref, r_ref, k_ref, v_ref, n_ref, a_ref, lw_ref, y_ref, p_ref, *, tt, group):
    c = CHUNK
    w = V7X_LANES
    n_slabs = RWKV_DIM // w

    @pl.when(pl.program_id(1) == 0)
    def _():
        zbuf_ref[0:HALO, :] = jnp.zeros((HALO, RWKV_IN), F32)
        p_ref[...] = jnp.zeros_like(p_ref)

    m = _pair_masks()
    ones_bd = jnp.where(m["bd"], 1.0, 0.0).astype(BF16)
    not_first = jnp.logical_not(m["first"])

    zbuf_ref[HALO:HALO + tt, :] = z_ref[...]

    def shifted(lo, hi):
        cur = zbuf_ref[HALO:HALO + tt, lo:hi]
        prev = zbuf_ref[HALO - 1:HALO - 1 + tt, lo:hi]
        return cur + mu_ref[:, lo:hi] * (prev - cur)

    lora = shifted(3 * RWKV_DIM, 3 * RWKV_DIM + DECAY_LORA + ICLR_LORA)
    lora_t = jnp.tanh(lora).astype(BF16)
    lora_b = lora.astype(BF16)
    gate_in = jax.nn.sigmoid(shifted(RWKV_IN - GATE_LORA, RWKV_IN)).astype(BF16)
    for s in range(n_slabs):
        sl = slice(s * w, (s + 1) * w)
        r = shifted(s * w, (s + 1) * w)
        k = shifted(RWKV_DIM + s * w, RWKV_DIM + (s + 1) * w)
        v = shifted(2 * RWKV_DIM + s * w, 2 * RWKV_DIM + (s + 1) * w)
        lw = -RWKV_DECAY_SCALE * jax.nn.sigmoid(w0_ref[:, sl] + _dot(lora_t, w2_ref[:, sl]))
        a = jax.nn.sigmoid(a0_ref[:, sl] + _dot(lora_b, a2_ref[:, sl]))
        kn = k * kk_ref[:, sl]
        kn = kn * lax.rsqrt(_seg_sum(kn * kn, ones_bd) + 1e-12)
        r_ref[:, sl] = r
        k_ref[:, sl] = k * (1.0 + (a - 1.0) * ka_ref[:, sl])
        v_ref[:, sl] = v
        n_ref[:, sl] = kn
        a_ref[:, sl] = a
        lw_ref[:, sl] = lw
    zbuf_ref[0:HALO, :] = zbuf_ref[tt:tt + HALO, :]

    def chunk_group(row_slices):
        slabs = range(n_slabs)
        sls = [slice(s * w, (s + 1) * w) for s in slabs]
        probs = [(rows, sl) for rows in row_slices for sl in sls]
        n = range(len(probs))
        bd = lambda x: _block_diag(x, m["bd"])
        r = [r_ref[rows, sl] for rows, sl in probs]
        k = [k_ref[rows, sl] for rows, sl in probs]
        v = [v_ref[rows, sl] for rows, sl in probs]
        kn = [n_ref[rows, sl] for rows, sl in probs]
        a = [a_ref[rows, sl] for rows, sl in probs]
        lw = [lw_ref[rows, sl] for rows, sl in probs]
        cum = [_cumsum_rows(x, m["row"]) for x in lw]
        b = [kn[i] * a[i] for i in n]
        at = [-kn[i] * jnp.exp(cum[i] - lw[i]) for i in n]
        rt = [(r[i] * jnp.exp(cum[i])).astype(BF16) for i in n]
        e_out = [jnp.exp(-x) for x in cum]
        bt = [b[i] * e_out[i] for i in n]
        kt = [k[i] * e_out[i] for i in n]
        aa = []
        for i in n:
            lhs = jnp.concatenate([at[i].astype(BF16), rt[i]], axis=0)
            rhs = jnp.concatenate(
                [jnp.where(m["first"], bt[i], 0.0), jnp.where(not_first, bt[i], 0.0),
                 jnp.where(m["first"], kt[i], 0.0), jnp.where(not_first, kt[i], 0.0)],
                axis=0).astype(BF16)
            aa.append(_dot_nt(lhs, rhs))
        a_ab = [jnp.where(m["strict"], x[0:c, 0:w], 0.0) for x in aa]
        a_ak = [jnp.where(m["strict"], x[0:c, w:2 * w], 0.0).astype(BF16) for x in aa]
        a_rb = [jnp.where(m["incl"], x[c:2 * c, 0:w], 0.0).astype(BF16) for x in aa]
        a_rk = [jnp.where(m["incl"], x[c:2 * c, w:2 * w], 0.0).astype(BF16) for x in aa]
        v_bd = [bd(x) for x in v]
        av = [_dot(jnp.concatenate([a_ak[i], a_rk[i]], axis=0), v_bd[i]) for i in n]
        t = _pair_inverse_many(a_ab, m["eye"], m["bd"])
        tu = [_dot(t[i].astype(BF16), jnp.concatenate([bd(av[i][0:c]), bd(at[i])], axis=1))
              for i in n]
        lhs_p = [jnp.concatenate([tu[i][:, w:2 * w].astype(BF16), rt[i]], axis=0) for i in n]
        kb_t = []
        d_col = []
        for i in n:
            d_end = jnp.exp(cum[i][c - 1:c, :]) * e_out[i]
            kb_t.append(jnp.concatenate([k[i] * d_end, b[i] * d_end], axis=0).T.astype(BF16))
            d_col.append(jnp.exp(cum[i].T[:, c - 1:c]))
        p_mat = [p_ref[s] for s in slabs]
        for j, rows in enumerate(row_slices):
            ids = [j * n_slabs + s for s in slabs]
            p_b = [x.astype(BF16) for x in p_mat]
            sp = [_dot(lhs_p[i], p_b[s]) for s, i in enumerate(ids)]
            u = [tu[i][:, 0:w] + sp[s][0:c] for s, i in enumerate(ids)]
            upd = [_dot(kb_t[i], jnp.concatenate([v[i], u[s]], axis=0).astype(BF16))
                   for s, i in enumerate(ids)]
            p_mat = [d_col[i] * p_mat[s] + jnp.where(m["bd"], upd[s], 0.0) for s, i in enumerate(ids)]
            for s, i in enumerate(ids):
                y_ref[rows, sls[s]] = av[i][c:2 * c] + sp[s][c:2 * c] + _dot(a_rb[i], bd(u[s]))
        for s in slabs:
            p_ref[s] = p_mat[s]

    n_chunks = tt // c
    if group == n_chunks:
        chunk_group([slice(j * c, (j + 1) * c) for j in range(n_chunks)])
    else:
        def body(gi, carry):
            base = gi * (group * c)
            chunk_group([pl.ds(pl.multiple_of(base + j * c, c), c) for j in range(group)])
            return carry
        lax.fori_loop(0, n_chunks // group, body, 0)

    inv_n = 1.0 / RWKV_HEAD_DIM
    for s in range(n_slabs):
        sl = slice(s * w, (s + 1) * w)
        y = y_ref[:, sl]
        mean = _seg_sum(y, ones_bd) * inv_n
        yc = y - mean
        var = _seg_sum(yc * yc, ones_bd) * inv_n
        y = yc * lax.rsqrt(var + RWKV_GN_EPS) * lng_ref[:, sl] + lnb_ref[:, sl]
        bonus = _seg_sum(r_ref[:, sl] * k_ref[:, sl] * rk_ref[:, sl], ones_bd) * v_ref[:, sl]
        g = _dot(gate_in, g2_ref[:, sl])
        o_ref[:, sl] = ((y + bonus) * g).astype(o_ref.dtype)


def _rwkv_mixer(z, mu, w0, w2_pad, a0, a2_pad, g2, k_k, k_a, r_k, ln_g, ln_b, *, tt=512, group=8):
    b, t, _ = z.shape
    lora = DECAY_LORA + ICLR_LORA
    const2 = lambda shape: pl.BlockSpec(shape, lambda i, j: (0, 0))
    vec = lambda a: a.reshape(1, RWKV_DIM)
    tile = pltpu.VMEM((tt, RWKV_DIM), F32)
    return pl.pallas_call(
        functools.partial(_rwkv_kernel, tt=tt, group=group),
        grid=(b, t // tt),
        in_specs=[pl.BlockSpec((None, tt, RWKV_IN), lambda i, j: (i, j, 0)),
                  const2((1, RWKV_IN)), const2((1, RWKV_DIM)), const2((lora, RWKV_DIM)),
                  const2((1, RWKV_DIM)), const2((lora, RWKV_DIM)), const2((GATE_LORA, RWKV_DIM))]
                 + [const2((1, RWKV_DIM))] * 5,
        out_specs=pl.BlockSpec((None, tt, RWKV_DIM), lambda i, j: (i, j, 0)),
        out_shape=jax.ShapeDtypeStruct((b, t, RWKV_DIM), BF16),
        scratch_shapes=[pltpu.VMEM((tt + HALO, RWKV_IN), F32)] + [tile] * 7
                       + [pltpu.VMEM((RWKV_DIM // V7X_LANES, V7X_LANES, V7X_LANES), F32)],
        compiler_params=_params(("parallel", "arbitrary")),
        name="rwkv7",
    )(z, mu.reshape(1, RWKV_IN), vec(w0), w2_pad, vec(a0), a2_pad, g2, vec(k_k), vec(k_a),
      vec(r_k), vec(ln_g), vec(ln_b))


def _block_diag_weights(w):
    per = (LRU_DIM // 2) // LRU_BLOCK_DIM
    out = jnp.zeros((2, LRU_DIM // 2, LRU_DIM // 2), F32)
    for blk in range(LRU_BLOCKS):
        p, i = divmod(blk, per)
        lo = i * LRU_BLOCK_DIM
        out = out.at[p, lo:lo + LRU_BLOCK_DIM, lo:lo + LRU_BLOCK_DIM].set(w[blk])
    return out.astype(BF16)


def kernel(x, norm1_g, ffn1_wg, ffn1_wu, ffn1_wd, norm_mix_g, w_in, rw_mu, rw_w0, rw_w2, rw_a0, rw_a2, rw_g2, rw_kk, rw_ka, rw_rk, rw_ln_g, rw_ln_b, gd_conv_w, gd_a_log, gd_dt_bias, gd_norm_g, lr_conv_w, lr_conv_b, lr_wa, lr_ba, lr_wx, lr_bx, lr_lam, w_out, norm2_g, ffn2_wg, ffn2_wu, ffn2_wd, final_g):
    b, t, d = x.shape
    n = b * t
    bf = lambda a: a.astype(BF16)
    xs = x.reshape(n, d)
    for l in range(DEPTH):
        xs = _ffn(xs, norm1_g[l], bf(ffn1_wg[l]), bf(ffn1_wu[l]), bf(ffn1_wd[l]))

        w_l = w_in[l]
        w_rw = bf(w_l[:, :RWKV_IN])
        w_gd = bf(jnp.pad(w_l[:, RWKV_IN:RWKV_IN + GDN_IN], ((0, 0), (0, GDN_Z - GDN_IN))))
        w_lr = bf(w_l[:, RWKV_IN + GDN_IN:])
        z_rw = _norm_proj(xs, norm_mix_g[l], w_rw, name="in_proj_rwkv").reshape(b, t, RWKV_IN)
        z_gd = _norm_proj(xs, norm_mix_g[l], w_gd, name="in_proj_gdn").reshape(b, t, GDN_Z)
        z_lr = _norm_proj(xs, norm_mix_g[l], w_lr, name="in_proj_lru").reshape(b, t, LRU_IN)

        w2_pad = bf(jnp.concatenate([rw_w2[l], jnp.zeros((ICLR_LORA, RWKV_DIM), F32)], axis=0))
        a2_pad = bf(jnp.concatenate([jnp.zeros((DECAY_LORA, RWKV_DIM), F32), rw_a2[l]], axis=0))
        y_rw = _rwkv_mixer(z_rw, rw_mu[l], rw_w0[l], w2_pad, rw_a0[l], a2_pad, bf(rw_g2[l]),
                           rw_kk[l], rw_ka[l], rw_rk[l].reshape(RWKV_DIM), rw_ln_g[l], rw_ln_b[l])

        lane_pad = lambda a: jnp.pad(a, (GDN_HEADS, GDN_SMALL - 2 * GDN_HEADS)).reshape(1, GDN_SMALL)
        y_gd = _gdn_mixer(z_gd, gd_conv_w[l], lane_pad(gd_a_log[l]), lane_pad(gd_dt_bias[l]),
                          gd_norm_g[l])

        y_lr = _lru_mixer(z_lr, lr_conv_w[l], lr_conv_b[l], _block_diag_weights(lr_wa[l]), lr_ba[l],
                          _block_diag_weights(lr_wx[l]), lr_bx[l], lr_lam[l])

        w_o = bf(w_out[l])
        xs = _out_proj(xs, y_rw.reshape(n, RWKV_DIM), y_gd.reshape(n, GDN_DIM),
                       y_lr.reshape(n, LRU_DIM), w_o[:RWKV_DIM], w_o[RWKV_DIM:RWKV_DIM + GDN_DIM],
                       w_o[RWKV_DIM + GDN_DIM:])
        last = l == DEPTH - 1
        xs = _ffn(xs, norm2_g[l], bf(ffn2_wg[l]), bf(ffn2_wu[l]), bf(ffn2_wd[l]),
                  final_g if last else None)
    return xs.reshape(b, t, d)
```

```python
import functools
import math

import jax
import jax.numpy as jnp
from jax import lax
from jax.experimental import pallas as pl
from jax.experimental.pallas import tpu as pltpu

F32 = jnp.float32
BF16 = jnp.bfloat16

D_MODEL = 2048
DEPTH = 2
CHUNK = 64
RMS_EPS = 1e-6
D_FF = 5632
RWKV_HEADS = 12
RWKV_HEAD_DIM = 64
RWKV_DIM = RWKV_HEADS * RWKV_HEAD_DIM
DECAY_LORA = 64
ICLR_LORA = 64
GATE_LORA = 128
RWKV_GN_EPS = 64e-5
RWKV_DECAY_SCALE = math.exp(-0.5)
GDN_HEADS = 6
GDN_HEAD_DIM = 128
GDN_DIM = GDN_HEADS * GDN_HEAD_DIM
GDN_CONV = 4
LRU_BLOCKS = 8
LRU_BLOCK_DIM = 64
LRU_DIM = LRU_BLOCKS * LRU_BLOCK_DIM
LRU_CONV = 4
LRU_C = 8.0
RWKV_IN = 3 * RWKV_DIM + DECAY_LORA + ICLR_LORA + GATE_LORA
GDN_IN = 4 * GDN_DIM + 2 * GDN_HEADS
LRU_IN = 2 * LRU_DIM

V7X_LANES = 128
V7X_SUBLANES = 8
V7X_VMEM_BYTES = 64 * 1024 * 1024
VMEM_LIMIT_BYTES = V7X_VMEM_BYTES - 8 * 1024 * 1024

GDN_SMALL = V7X_LANES
GDN_Z = 4 * GDN_DIM + GDN_SMALL
HALO = V7X_SUBLANES
FFN_ROW_SLAB = 256
FFN_COL_SLAB = 512


def _params(sem, vmem_bytes=VMEM_LIMIT_BYTES):
    return pltpu.CompilerParams(dimension_semantics=sem,
                                vmem_limit_bytes=min(vmem_bytes, V7X_VMEM_BYTES))


def _nbytes(shape, dtype):
    return math.prod(shape) * jnp.dtype(dtype).itemsize


def _dot(a, b, precision=None):
    return jnp.dot(a, b, preferred_element_type=F32, precision=precision)


def _dot_nt(a, b):
    return lax.dot_general(a, b, (((1,), (1,)), ((), ())), preferred_element_type=F32)


def _silu(x):
    return x * jax.nn.sigmoid(x)


def _softplus(x):
    return jnp.maximum(x, 0.0) + jnp.log1p(jnp.exp(-jnp.abs(x)))


def _rms(x, g):
    return x * lax.rsqrt(jnp.mean(x * x, axis=-1, keepdims=True) + RMS_EPS) * g


def _ffn_kernel(*refs, nf, final):
    if final:
        x_ref, g_ref, wg_ref, wu_ref, wd_ref, fg_ref, o_ref, h_ref = refs
    else:
        x_ref, g_ref, wg_ref, wu_ref, wd_ref, o_ref, h_ref = refs
    j = pl.program_id(1)
    tm, d = o_ref.shape
    def for_row_slabs(fn):
        def body(r, carry):
            fn(pl.ds(pl.multiple_of(r * FFN_ROW_SLAB, FFN_ROW_SLAB), FFN_ROW_SLAB))
            return carry
        lax.fori_loop(0, tm // FFN_ROW_SLAB, body, 0)

    @pl.when(j == 0)
    def _():
        def norm(rs):
            h_ref[rs, :] = _rms(x_ref[rs, :], g_ref[...]).astype(BF16)
            o_ref[rs, :] = jnp.zeros((FFN_ROW_SLAB, d), F32)
        for_row_slabs(norm)

    h = h_ref[...]
    a = _dot(h, wg_ref[...])
    u = _dot(h, wu_ref[...])
    act = (_silu(a) * u).astype(BF16)
    for c in range(0, d, FFN_COL_SLAB):
        o_ref[:, c:c + FFN_COL_SLAB] += _dot(act, wd_ref[:, c:c + FFN_COL_SLAB])

    @pl.when(j == nf - 1)
    def _():
        def finish(rs):
            y = x_ref[rs, :] + 0.5 * o_ref[rs, :]
            if final:
                y = _rms(y, fg_ref[...])
            o_ref[rs, :] = y
        for_row_slabs(finish)


def _ffn(x, g, wg, wu, wd, final_g=None, *, tm=1024, tf=512):
    n, d = x.shape
    f = wg.shape[1]
    final = final_g is not None
    row = lambda i, j: (0, 0)
    in_specs = [
        pl.BlockSpec((tm, d), lambda i, j: (i, 0)),
        pl.BlockSpec((1, d), row),
        pl.BlockSpec((d, tf), lambda i, j: (0, j)),
        pl.BlockSpec((d, tf), lambda i, j: (0, j)),
        pl.BlockSpec((tf, d), lambda i, j: (j, 0)),
    ]
    args = [x, g.reshape(1, d), wg, wu, wd]
    if final:
        in_specs.append(pl.BlockSpec((1, d), row))
        args.append(final_g.reshape(1, d))
    vmem = (2 * (2 * _nbytes((tm, d), F32) + 3 * _nbytes((d, tf), BF16)) + _nbytes((tm, d), BF16)
            + 2 * (2 * _nbytes((tm, tf), F32) + _nbytes((tm, tf), BF16) + _nbytes((tm, FFN_COL_SLAB), F32)))
    return pl.pallas_call(
        functools.partial(_ffn_kernel, nf=f // tf, final=final),
        grid=(n // tm, f // tf),
        in_specs=in_specs,
        out_specs=pl.BlockSpec((tm, d), lambda i, j: (i, 0)),
        out_shape=jax.ShapeDtypeStruct((n, d), F32),
        scratch_shapes=[pltpu.VMEM((tm, d), BF16)],
        compiler_params=_params(("parallel", "arbitrary"), vmem),
        name="ffn",
    )(*args)


def _norm_proj_kernel(x_ref, g_ref, w_ref, o_ref):
    h = _rms(x_ref[...], g_ref[...]).astype(BF16)
    o_ref[...] = _dot(h, w_ref[...])


def _norm_proj(x, g, w, *, tm=512, name):
    n, d = x.shape
    c = w.shape[1]
    return pl.pallas_call(
        _norm_proj_kernel,
        grid=(n // tm,),
        in_specs=[
            pl.BlockSpec((tm, d), lambda i: (i, 0)),
            pl.BlockSpec((1, d), lambda i: (0, 0)),
            pl.BlockSpec((d, c), lambda i: (0, 0)),
        ],
        out_specs=pl.BlockSpec((tm, c), lambda i: (i, 0)),
        out_shape=jax.ShapeDtypeStruct((n, c), F32),
        compiler_params=_params(("parallel",)),
        name=name,
    )(x, g.reshape(1, d), w)


def _out_proj_kernel(x_ref, yr_ref, yg_ref, yl_ref, wr_ref, wg_ref, wl_ref, o_ref):
    o_ref[...] = (x_ref[...] + _dot(yr_ref[...], wr_ref[...]) + _dot(yg_ref[...], wg_ref[...])
                  + _dot(yl_ref[...], wl_ref[...]))


def _out_proj(x, y_rw, y_gd, y_lr, w_rw, w_gd, w_lr, *, tm=512):
    n, d = x.shape
    tile = lambda c: pl.BlockSpec((tm, c), lambda i: (i, 0))
    whole = lambda c: pl.BlockSpec((c, d), lambda i: (0, 0))
    return pl.pallas_call(
        _out_proj_kernel,
        grid=(n // tm,),
        in_specs=[tile(d), tile(RWKV_DIM), tile(GDN_DIM), tile(LRU_DIM),
                  whole(RWKV_DIM), whole(GDN_DIM), whole(LRU_DIM)],
        out_specs=tile(d),
        out_shape=jax.ShapeDtypeStruct((n, d), F32),
        compiler_params=_params(("parallel",)),
        name="out_proj",
    )(x, y_rw, y_gd, y_lr, w_rw, w_gd, w_lr)


def _shift_rows(x, d, fill, row):
    return jnp.where(row >= d, pltpu.roll(x, d, 0), fill)


def _cumsum_rows(x, row):
    d = 1
    while d < x.shape[0]:
        x = x + _shift_rows(x, d, 0.0, row)
        d *= 2
    return x


def _block_diag(m, bd_mask):
    mb = m.astype(BF16)
    return jnp.where(bd_mask, jnp.concatenate([mb, mb], axis=0), jnp.zeros((), BF16))


def _pair_inverse_many(xs, eye2, bd_mask):
    c = CHUNK
    bd = lambda x: _block_diag(x, bd_mask)
    ss = [eye2 + x for x in xs]
    ps = [_dot(x.astype(BF16), bd(x)) for x in xs]
    for _ in range(c.bit_length() - 3):
        prods = [_dot(jnp.concatenate([p, s], axis=0).astype(BF16), bd(p)) for p, s in zip(ps, ss)]
        ps = [x[0:c] for x in prods]
        ss = [s + x[c:2 * c] for s, x in zip(ss, prods)]
    return [s + _dot(s.astype(BF16), bd(p)) for p, s in zip(ps, ss)]


def _seg_sum(x, ones_bd):
    hi = x.astype(BF16)
    lo = (x - hi.astype(F32)).astype(BF16)
    return _dot(hi, ones_bd) + _dot(lo, ones_bd)


def _pair_masks():
    c = CHUNK
    row = lax.broadcasted_iota(jnp.int32, (c, 2 * c), 0)
    lane = lax.broadcasted_iota(jnp.int32, (c, 2 * c), 1)
    col = jnp.where(lane >= c, lane - c, lane)
    r2 = lax.broadcasted_iota(jnp.int32, (2 * c, 2 * c), 0)
    l2 = lax.broadcasted_iota(jnp.int32, (2 * c, 2 * c), 1)
    bd_mask = (r2 >= c) == (l2 >= c)
    return dict(row=row, first=lane < c, strict=row > col, incl=row >= col,
                eye=(row == col).astype(F32), bd=bd_mask)


def _lru_kernel(z_ref, cw_ref, cb_ref, wa_ref, ba_ref, wx_ref, bx_ref, lam_ref, o_ref,
                xbuf_ref, h_ref, *, tt):
    @pl.when(pl.program_id(1) == 0)
    def _():
        xbuf_ref[0:HALO, :] = jnp.zeros((HALO, LRU_DIM), F32)
        h_ref[...] = jnp.zeros_like(h_ref)

    xl = z_ref[:, 0:LRU_DIM]
    yl = z_ref[:, LRU_DIM:2 * LRU_DIM]
    xbuf_ref[HALO:HALO + tt, :] = xl
    xc = cb_ref[...] + cw_ref[LRU_CONV - 1:LRU_CONV, :] * xl
    for j in range(1, LRU_CONV):
        xc = xc + cw_ref[LRU_CONV - 1 - j:LRU_CONV - j, :] * xbuf_ref[HALO - j:HALO - j + tt, :]
    xbuf_ref[0:HALO, :] = xbuf_ref[tt:tt + HALO, :]

    xcb = xc.astype(BF16)
    half = LRU_DIM // 2
    ra = jnp.concatenate([_dot(xcb[:, p * half:(p + 1) * half], wa_ref[p]) for p in range(2)], axis=1)
    ia = jnp.concatenate([_dot(xcb[:, p * half:(p + 1) * half], wx_ref[p]) for p in range(2)], axis=1)
    r = jax.nn.sigmoid(ra + ba_ref[...])
    i = jax.nn.sigmoid(ia + bx_ref[...])
    log_a = -LRU_C * r * _softplus(-lam_ref[...])
    a = jnp.exp(log_a)
    u = jnp.sqrt(1.0 - a * a) * (i * xc)

    s = V7X_SUBLANES
    groups = tt // s
    a = a.reshape(groups, s, LRU_DIM)
    u = u.reshape(groups, s, LRU_DIM)
    sub = lax.broadcasted_iota(jnp.int32, (groups, s, LRU_DIM), 1)
    d = 1
    while d < s:
        u = a * jnp.where(sub >= d, pltpu.roll(u, d, 1), 0.0) + u
        a = a * jnp.where(sub >= d, pltpu.roll(a, d, 1), 1.0)
        d *= 2
    carry = h_ref[...]
    hs = []
    for g in range(groups):
        hs.append(u[g] + a[g] * carry)
        carry = hs[-1][s - 1:s, :]
    h_ref[...] = carry
    o_ref[...] = (jnp.concatenate(hs, axis=0) * jax.nn.gelu(yl)).astype(o_ref.dtype)


def _lru_mixer(z, conv_w, conv_b, wa_bd, b_a, wx_bd, b_x, lam, *, tt=512):
    b, t, _ = z.shape
    vec = lambda a: a.reshape(1, LRU_DIM)
    const2 = lambda shape: pl.BlockSpec(shape, lambda i, j: (0, 0))
    const3 = lambda shape: pl.BlockSpec(shape, lambda i, j: (0, 0, 0))
    half = LRU_DIM // 2
    return pl.pallas_call(
        functools.partial(_lru_kernel, tt=tt),
        grid=(b, t // tt),
        in_specs=[pl.BlockSpec((None, tt, LRU_IN), lambda i, j: (i, j, 0)),
                  const2((LRU_CONV, LRU_DIM)), const2((1, LRU_DIM)),
                  const3((2, half, half)), const2((1, LRU_DIM)),
                  const3((2, half, half)), const2((1, LRU_DIM)), const2((1, LRU_DIM))],
        out_specs=pl.BlockSpec((None, tt, LRU_DIM), lambda i, j: (i, j, 0)),
        out_shape=jax.ShapeDtypeStruct((b, t, LRU_DIM), BF16),
        scratch_shapes=[pltpu.VMEM((tt + HALO, LRU_DIM), F32), pltpu.VMEM((1, LRU_DIM), F32)],
        compiler_params=_params(("parallel", "arbitrary")),
        name="rglru",
    )(z, conv_w, vec(conv_b), wa_bd, vec(b_a), wx_bd, vec(b_x), vec(lam))


def _gdn_kernel(z_ref, cw_ref, alog_ref, dtb_ref, ng_ref, o_ref,
                xbuf_ref, q_ref, k_ref, v_ref, beta_ref, g_ref, s_ref, *, tt, group):
    c = CHUNK
    hd = GDN_HEAD_DIM
    qkv_dim = 3 * GDN_DIM

    @pl.when(pl.program_id(1) == 0)
    def _():
        xbuf_ref[0:HALO, :] = jnp.zeros((HALO, qkv_dim), F32)
        s_ref[...] = jnp.zeros_like(s_ref)

    xbuf_ref[HALO:HALO + tt, :] = z_ref[:, 0:qkv_dim]
    for s in range(3 * GDN_HEADS):
        sl = slice(s * hd, (s + 1) * hd)
        acc = cw_ref[GDN_CONV - 1:GDN_CONV, sl] * xbuf_ref[HALO:HALO + tt, sl]
        for j in range(1, GDN_CONV):
            acc = acc + cw_ref[GDN_CONV - 1 - j:GDN_CONV - j, sl] * xbuf_ref[HALO - j:HALO - j + tt, sl]
        y = _silu(acc)
        which, head = divmod(s, GDN_HEADS)
        if which < 2:
            y = y * lax.rsqrt(jnp.sum(y * y, axis=-1, keepdims=True) + 1e-12)
        if which == 0:
            y = y * (hd ** -0.5)
        (q_ref, k_ref, v_ref)[which][:, head * hd:(head + 1) * hd] = y
    xbuf_ref[0:HALO, :] = xbuf_ref[tt:tt + HALO, :]

    small = z_ref[:, 4 * GDN_DIM:4 * GDN_DIM + GDN_SMALL]
    beta_ref[...] = jax.nn.sigmoid(small)
    g_ref[...] = -jnp.exp(alog_ref[...]) * _softplus(small + dtb_ref[...])

    m = _pair_masks()
    row_s = lax.broadcasted_iota(jnp.int32, (c, GDN_SMALL), 0)
    ng = ng_ref[...]

    def chunk_group(row_slices):
        nh = GDN_HEADS
        chunks = range(len(row_slices))
        hp = [(j, h) for j in chunks for h in range(nh)]
        pp = [(j, p) for j in chunks for p in range(nh // 2)]
        sls = [slice(h * hd, (h + 1) * hd) for h in range(nh)]
        sel = [m["first"], jnp.logical_not(m["first"])]
        gc = [_cumsum_rows(g_ref[rows, :], row_s) for rows in row_slices]
        gct = [x.T for x in gc]
        beta = [beta_ref[rows, :] for rows in row_slices]
        gcol = [gc[j][:, nh + h:nh + h + 1] for j, h in hp]
        grow = [gct[j][nh + h:nh + h + 1, :] for j, h in hp]
        bcol = [beta[j][:, h:h + 1] for j, h in hp]
        q = [q_ref[row_slices[j], sls[h]] for j, h in hp]
        k = [k_ref[row_slices[j], sls[h]] for j, h in hp]
        v = [v_ref[row_slices[j], sls[h]] for j, h in hp]
        n = range(len(hp))
        kb = [k[i] * bcol[i] for i in n]
        egc = [jnp.exp(gcol[i]) for i in n]
        zero = jnp.zeros((c, hd), BF16)
        kbf = [x.astype(BF16) for x in k]
        mm = []
        qk = []
        for j, p in pp:
            i0, i1 = j * nh + 2 * p, j * nh + 2 * p + 1
            gcol2 = jnp.where(m["first"], gcol[i0], gcol[i1])
            grow2 = jnp.concatenate([grow[i0], grow[i1]], axis=1)
            decay = jnp.where(m["incl"], jnp.exp(jnp.where(m["incl"], gcol2 - grow2, 0.0)), 0.0)
            lhs = jnp.concatenate([jnp.concatenate([kb[i0], kb[i1]], axis=1),
                                   jnp.concatenate([q[i0], q[i1]], axis=1)], axis=0).astype(BF16)
            rhs = jnp.concatenate([jnp.concatenate([kbf[i0], zero], axis=1),
                                   jnp.concatenate([zero, kbf[i1]], axis=1)], axis=0)
            kq = _dot_nt(lhs, rhs)
            mm.append(jnp.where(m["strict"], -(kq[0:c, :] * decay), 0.0))
            qk.append((kq[c:2 * c, :] * decay))
        qk2 = [jnp.concatenate([jnp.where(sel[0], x, 0.0), jnp.where(sel[1], x, 0.0)], axis=0).astype(BF16)
               for x in qk]
        t = _pair_inverse_many(mm, m["eye"], m["bd"])
        rhs_sol = [jnp.concatenate(
            [jnp.concatenate([v[i] * bcol[i], kb[i] * egc[i]], axis=1)
             for i in (j * nh + 2 * p, j * nh + 2 * p + 1)], axis=0).astype(BF16)
            for j, p in pp]
        sol = [_dot(jnp.where(sel[i % 2], t[i // 2], 0.0).astype(BF16), rhs_sol[i // 2])
               for i in n]
        qe = [(q[i] * egc[i]).astype(BF16) for i in n]
        g_last = [gcol[i][c - 1:c, :] for i in n]
        kd = [(k[i] * jnp.exp(g_last[i] - gcol[i])).T.astype(BF16) for i in n]
        s_mat = [s_ref[h] for h in range(nh)]
        for j in chunks:
            ids = [j * nh + h for h in range(nh)]
            s_b = [x.astype(BF16) for x in s_mat]
            ws = [_dot(jnp.concatenate([sol[i][:, hd:2 * hd].astype(BF16), qe[i]], axis=0), s_b[h])
                  for h, i in enumerate(ids)]
            vnb = [(sol[i][:, 0:hd] - ws[h][0:c]).astype(BF16) for h, i in enumerate(ids)]
            s_mat = [s_mat[h] * jnp.exp(g_last[i]) + _dot(kd[i], vnb[h]) for h, i in enumerate(ids)]
            qv = [_dot(qk2[j * (nh // 2) + p], jnp.concatenate([vnb[2 * p], vnb[2 * p + 1]], axis=0))
                  for p in range(nh // 2)]
            for h, i in enumerate(ids):
                oo = ws[h][c:2 * c] + qv[h // 2][(h % 2) * c:(h % 2 + 1) * c]
                gate = z_ref[row_slices[j], 3 * GDN_DIM + h * hd:3 * GDN_DIM + (h + 1) * hd]
                o_ref[row_slices[j], sls[h]] = (_rms(oo, ng) * _silu(gate)).astype(o_ref.dtype)
        for h in range(nh):
            s_ref[h] = s_mat[h]

    n_chunks = tt // c
    if group == n_chunks:
        chunk_group([slice(j * c, (j + 1) * c) for j in range(n_chunks)])
    else:
        def body(gi, carry):
            base = gi * (group * c)
            chunk_group([pl.ds(pl.multiple_of(base + j * c, c), c) for j in range(group)])
            return carry
        lax.fori_loop(0, n_chunks // group, body, 0)


def _gdn_mixer(z, conv_w, a_log_pad, dt_bias_pad, norm_g, *, tt=512, group=8):
    b, t, _ = z.shape
    const2 = lambda shape: pl.BlockSpec(shape, lambda i, j: (0, 0))
    return pl.pallas_call(
        functools.partial(_gdn_kernel, tt=tt, group=group),
        grid=(b, t // tt),
        in_specs=[pl.BlockSpec((None, tt, GDN_Z), lambda i, j: (i, j, 0)),
                  const2((GDN_CONV, 3 * GDN_DIM)), const2((1, GDN_SMALL)), const2((1, GDN_SMALL)),
                  const2((1, GDN_HEAD_DIM))],
        out_specs=pl.BlockSpec((None, tt, GDN_DIM), lambda i, j: (i, j, 0)),
        out_shape=jax.ShapeDtypeStruct((b, t, GDN_DIM), BF16),
        scratch_shapes=[pltpu.VMEM((tt + HALO, 3 * GDN_DIM), F32),
                        pltpu.VMEM((tt, GDN_DIM), F32), pltpu.VMEM((tt, GDN_DIM), F32),
                        pltpu.VMEM((tt, GDN_DIM), F32),
                        pltpu.VMEM((tt, GDN_SMALL), F32), pltpu.VMEM((tt, GDN_SMALL), F32),
                        pltpu.VMEM((GDN_HEADS, GDN_HEAD_DIM, GDN_HEAD_DIM), F32)],
        compiler_params=_params(("parallel", "arbitrary")),
        name="gdn",
    )(z, conv_w, a_log_pad, dt_bias_pad, norm_g.reshape(1, GDN_HEAD_DIM))


def _rwkv_kernel(z_ref, mu_ref, w0_ref, w2_ref, a0_ref, a2_ref, g2_ref, kk_ref, ka_ref, rk_ref,
                 lng_ref, lnb_ref, o_ref,
                 zbuf_ref, r_ref, k_ref, v_ref, n_ref, a_ref, lw_ref, y_ref, p_ref, *, tt, group):
    c = CHUNK
    w = V7X_LANES
    n_slabs = RWKV_DIM // w

    @pl.when(pl.program_id(1) == 0)
    def _():
        zbuf_ref[0:HALO, :] = jnp.zeros((HALO, RWKV_IN), F32)
        p_ref[...] = jnp.zeros_like(p_ref)

    m = _pair_masks()
    ones_bd = jnp.where(m["bd"], 1.0, 0.0).astype(BF16)
    not_first = jnp.logical_not(m["first"])

    zbuf_ref[HALO:HALO + tt, :] = z_ref[...]

    def shifted(lo, hi):
        cur = zbuf_ref[HALO:HALO + tt, lo:hi]
        prev = zbuf_ref[HALO - 1:HALO - 1 + tt, lo:hi]
        return cur + mu_ref[:, lo:hi] * (prev - cur)

    lora = shifted(3 * RWKV_DIM, 3 * RWKV_DIM + DECAY_LORA + ICLR_LORA)
    lora_t = jnp.tanh(lora).astype(BF16)
    lora_b = lora.astype(BF16)
    gate_in = jax.nn.sigmoid(shifted(RWKV_IN - GATE_LORA, RWKV_IN)).astype(BF16)
    for s in range(n_slabs):
        sl = slice(s * w, (s + 1) * w)
        r = shifted(s * w, (s + 1) * w)
        k = shifted(RWKV_DIM + s * w, RWKV_DIM + (s + 1) * w)
        v = shifted(2 * RWKV_DIM + s * w, 2 * RWKV_DIM + (s + 1) * w)
        lw = -RWKV_DECAY_SCALE * jax.nn.sigmoid(w0_ref[:, sl] + _dot(lora_t, w2_ref[:, sl]))
        a = jax.nn.sigmoid(a0_ref[:, sl] + _dot(lora_b, a2_ref[:, sl]))
        kn = k * kk_ref[:, sl]
        kn = kn * lax.rsqrt(_seg_sum(kn * kn, ones_bd) + 1e-12)
        r_ref[:, sl] = r
        k_ref[:, sl] = k * (1.0 + (a - 1.0) * ka_ref[:, sl])
        v_ref[:, sl] = v
        n_ref[:, sl] = kn
        a_ref[:, sl] = a
        lw_ref[:, sl] = lw
    zbuf_ref[0:HALO, :] = zbuf_ref[tt:tt + HALO, :]

    def chunk_group(row_slices):
        slabs = range(n_slabs)
        sls = [slice(s * w, (s + 1) * w) for s in slabs]
        probs = [(rows, sl) for rows in row_slices for sl in sls]
        n = range(len(probs))
        bd = lambda x: _block_diag(x, m["bd"])
        r = [r_ref[rows, sl] for rows, sl in probs]
        k = [k_ref[rows, sl] for rows, sl in probs]
        v = [v_ref[rows, sl] for rows, sl in probs]
        kn = [n_ref[rows, sl] for rows, sl in probs]
        a = [a_ref[rows, sl] for rows, sl in probs]
        lw = [lw_ref[rows, sl] for rows, sl in probs]
        cum = [_cumsum_rows(x, m["row"]) for x in lw]
        b = [kn[i] * a[i] for i in n]
        at = [-kn[i] * jnp.exp(cum[i] - lw[i]) for i in n]
        rt = [(r[i] * jnp.exp(cum[i])).astype(BF16) for i in n]
        e_out = [jnp.exp(-x) for x in cum]
        bt = [b[i] * e_out[i] for i in n]
        kt = [k[i] * e_out[i] for i in n]
        aa = []
        for i in n:
            lhs = jnp.concatenate([at[i].astype(BF16), rt[i]], axis=0)
            rhs = jnp.concatenate(
                [jnp.where(m["first"], bt[i], 0.0), jnp.where(not_first, bt[i], 0.0),
                 jnp.where(m["first"], kt[i], 0.0), jnp.where(not_first, kt[i], 0.0)],
                axis=0).astype(BF16)
            aa.append(_dot_nt(lhs, rhs))
        a_ab = [jnp.where(m["strict"], x[0:c, 0:w], 0.0) for x in aa]
        a_ak = [jnp.where(m["strict"], x[0:c, w:2 * w], 0.0).astype(BF16) for x in aa]
        a_rb = [jnp.where(m["incl"], x[c:2 * c, 0:w], 0.0).astype(BF16) for x in aa]
        a_rk = [jnp.where(m["incl"], x[c:2 * c, w:2 * w], 0.0).astype(BF16) for x in aa]
        v_bd = [bd(x) for x in v]
        av = [_dot(jnp.concatenate([a_ak[i], a_rk[i]], axis=0), v_bd[i]) for i in n]
        t = _pair_inverse_many(a_ab, m["eye"], m["bd"])
        tu = [_dot(t[i].astype(BF16), jnp.concatenate([bd(av[i][0:c]), bd(at[i])], axis=1))
              for i in n]
        lhs_p = [jnp.concatenate([tu[i][:, w:2 * w].astype(BF16), rt[i]], axis=0) for i in n]
        kb_t = []
        d_col = []
        for i in n:
            d_end = jnp.exp(cum[i][c - 1:c, :]) * e_out[i]
            kb_t.append(jnp.concatenate([k[i] * d_end, b[i] * d_end], axis=0).T.astype(BF16))
            d_col.append(jnp.exp(cum[i].T[:, c - 1:c]))
        p_mat = [p_ref[s] for s in slabs]
        for j, rows in enumerate(row_slices):
            ids = [j * n_slabs + s for s in slabs]
            p_b = [x.astype(BF16) for x in p_mat]
            sp = [_dot(lhs_p[i], p_b[s]) for s, i in enumerate(ids)]
            u = [tu[i][:, 0:w] + sp[s][0:c] for s, i in enumerate(ids)]
            upd = [_dot(kb_t[i], jnp.concatenate([v[i], u[s]], axis=0).astype(BF16))
                   for s, i in enumerate(ids)]
            p_mat = [d_col[i] * p_mat[s] + jnp.where(m["bd"], upd[s], 0.0) for s, i in enumerate(ids)]
            for s, i in enumerate(ids):
                y_ref[rows, sls[s]] = av[i][c:2 * c] + sp[s][c:2 * c] + _dot(a_rb[i], bd(u[s]))
        for s in slabs:
            p_ref[s] = p_mat[s]

    n_chunks = tt // c
    if group == n_chunks:
        chunk_group([slice(j * c, (j + 1) * c) for j in range(n_chunks)])
    else:
        def body(gi, carry):
            base = gi * (group * c)
            chunk_group([pl.ds(pl.multiple_of(base + j * c, c), c) for j in range(group)])
            return carry
        lax.fori_loop(0, n_chunks // group, body, 0)

    inv_n = 1.0 / RWKV_HEAD_DIM
    for s in range(n_slabs):
        sl = slice(s * w, (s + 1) * w)
        y = y_ref[:, sl]
        mean = _seg_sum(y, ones_bd) * inv_n
        yc = y - mean
        var = _seg_sum(yc * yc, ones_bd) * inv_n
        y = yc * lax.rsqrt(var + RWKV_GN_EPS) * lng_ref[:, sl] + lnb_ref[:, sl]
        bonus = _seg_sum(r_ref[:, sl] * k_ref[:, sl] * rk_ref[:, sl], ones_bd) * v_ref[:, sl]
        g = _dot(gate_in, g2_ref[:, sl])
        o_ref[:, sl] = ((y + bonus) * g).astype(o_ref.dtype)


def _rwkv_mixer(z, mu, w0, w2_pad, a0, a2_pad, g2, k_k, k_a, r_k, ln_g, ln_b, *, tt=512, group=8):
    b, t, _ = z.shape
    lora = DECAY_LORA + ICLR_LORA
    const2 = lambda shape: pl.BlockSpec(shape, lambda i, j: (0, 0))
    vec = lambda a: a.reshape(1, RWKV_DIM)
    tile = pltpu.VMEM((tt, RWKV_DIM), F32)
    return pl.pallas_call(
        functools.partial(_rwkv_kernel, tt=tt, group=group),
        grid=(b, t // tt),
        in_specs=[pl.BlockSpec((None, tt, RWKV_IN), lambda i, j: (i, j, 0)),
                  const2((1, RWKV_IN)), const2((1, RWKV_DIM)), const2((lora, RWKV_DIM)),
                  const2((1, RWKV_DIM)), const2((lora, RWKV_DIM)), const2((GATE_LORA, RWKV_DIM))]
                 + [const2((1, RWKV_DIM))] * 5,
        out_specs=pl.BlockSpec((None, tt, RWKV_DIM), lambda i, j: (i, j, 0)),
        out_shape=jax.ShapeDtypeStruct((b, t, RWKV_DIM), BF16),
        scratch_shapes=[pltpu.VMEM((tt + HALO, RWKV_IN), F32)] + [tile] * 7
                       + [pltpu.VMEM((RWKV_DIM // V7X_LANES, V7X_LANES, V7X_LANES), F32)],
        compiler_params=_params(("parallel", "arbitrary")),
        name="rwkv7",
    )(z, mu.reshape(1, RWKV_IN), vec(w0), w2_pad, vec(a0), a2_pad, g2, vec(k_k), vec(k_a),
      vec(r_k), vec(ln_g), vec(ln_b))


def _block_diag_weights(w):
    per = (LRU_DIM // 2) // LRU_BLOCK_DIM
    out = jnp.zeros((2, LRU_DIM // 2, LRU_DIM // 2), F32)
    for blk in range(LRU_BLOCKS):
        p, i = divmod(blk, per)
        lo = i * LRU_BLOCK_DIM
        out = out.at[p, lo:lo + LRU_BLOCK_DIM, lo:lo + LRU_BLOCK_DIM].set(w[blk])
    return out.astype(BF16)


def kernel(x, norm1_g, ffn1_wg, ffn1_wu, ffn1_wd, norm_mix_g, w_in, rw_mu, rw_w0, rw_w2, rw_a0, rw_a2, rw_g2, rw_kk, rw_ka, rw_rk, rw_ln_g, rw_ln_b, gd_conv_w, gd_a_log, gd_dt_bias, gd_norm_g, lr_conv_w, lr_conv_b, lr_wa, lr_ba, lr_wx, lr_bx, lr_lam, w_out, norm2_g, ffn2_wg, ffn2_wu, ffn2_wd, final_g):
    b, t, d = x.shape
    n = b * t
    bf = lambda a: a.astype(BF16)
    xs = x.reshape(n, d)
    for l in range(DEPTH):
        xs = _ffn(xs, norm1_g[l], bf(ffn1_wg[l]), bf(ffn1_wu[l]), bf(ffn1_wd[l]))

        w_l = w_in[l]
        w_rw = bf(w_l[:, :RWKV_IN])
        w_gd = bf(jnp.pad(w_l[:, RWKV_IN:RWKV_IN + GDN_IN], ((0, 0), (0, GDN_Z - GDN_IN))))
        w_lr = bf(w_l[:, RWKV_IN + GDN_IN:])
        z_rw = _norm_proj(xs, norm_mix_g[l], w_rw, name="in_proj_rwkv").reshape(b, t, RWKV_IN)
        z_gd = _norm_proj(xs, norm_mix_g[l], w_gd, name="in_proj_gdn").reshape(b, t, GDN_Z)
        z_lr = _norm_proj(xs, norm_mix_g[l], w_lr, name="in_proj_lru").reshape(b, t, LRU_IN)

        w2_pad = bf(jnp.concatenate([rw_w2[l], jnp.zeros((ICLR_LORA, RWKV_DIM), F32)], axis=0))
        a2_pad = bf(jnp.concatenate([jnp.zeros((DECAY_LORA, RWKV_DIM), F32), rw_a2[l]], axis=0))
        y_rw = _rwkv_mixer(z_rw, rw_mu[l], rw_w0[l], w2_pad, rw_a0[l], a2_pad, bf(rw_g2[l]),
                           rw_kk[l], rw_ka[l], rw_rk[l].reshape(RWKV_DIM), rw_ln_g[l], rw_ln_b[l])

        lane_pad = lambda a: jnp.pad(a, (GDN_HEADS, GDN_SMALL - 2 * GDN_HEADS)).reshape(1, GDN_SMALL)
        y_gd = _gdn_mixer(z_gd, gd_conv_w[l], lane_pad(gd_a_log[l]), lane_pad(gd_dt_bias[l]),
                          gd_norm_g[l])

        y_lr = _lru_mixer(z_lr, lr_conv_w[l], lr_conv_b[l], _block_diag_weights(lr_wa[l]), lr_ba[l],
                          _block_diag_weights(lr_wx[l]), lr_bx[l], lr_lam[l])

        w_o = bf(w_out[l])
        xs = _out_proj(xs, y_rw.reshape(n, RWKV_DIM), y_gd.reshape(n, GDN_DIM),
                       y_lr.reshape(n, LRU_DIM), w_o[:RWKV_DIM], w_o[RWKV_DIM:RWKV_DIM + GDN_DIM],
                       w_o[RWKV_DIM + GDN_DIM:])
        last = l == DEPTH - 1
        xs = _ffn(xs, norm2_g[l], bf(ffn2_wg[l]), bf(ffn2_wu[l]), bf(ffn2_wd[l]),
                  final_g if last else None)
    return xs.reshape(b, t, d)
```

```python
import functools
import math

import jax
import jax.numpy as jnp
from jax import lax
from jax.experimental import pallas as pl
from jax.experimental.pallas import tpu as pltpu

F32 = jnp.float32
BF16 = jnp.bfloat16

D_MODEL = 2048
DEPTH = 2
CHUNK = 64
RMS_EPS = 1e-6
D_FF = 5632
RWKV_HEADS = 12
RWKV_HEAD_DIM = 64
RWKV_DIM = RWKV_HEADS * RWKV_HEAD_DIM
DECAY_LORA = 64
ICLR_LORA = 64
GATE_LORA = 128
RWKV_GN_EPS = 64e-5
RWKV_DECAY_SCALE = math.exp(-0.5)
GDN_HEADS = 6
GDN_HEAD_DIM = 128
GDN_DIM = GDN_HEADS * GDN_HEAD_DIM
GDN_CONV = 4
LRU_BLOCKS = 8
LRU_BLOCK_DIM = 64
LRU_DIM = LRU_BLOCKS * LRU_BLOCK_DIM
LRU_CONV = 4
LRU_C = 8.0
RWKV_IN = 3 * RWKV_DIM + DECAY_LORA + ICLR_LORA + GATE_LORA
GDN_IN = 4 * GDN_DIM + 2 * GDN_HEADS
LRU_IN = 2 * LRU_DIM

V7X_LANES = 128
V7X_SUBLANES = 8
V7X_VMEM_BYTES = 64 * 1024 * 1024
VMEM_LIMIT_BYTES = V7X_VMEM_BYTES - 8 * 1024 * 1024

GDN_SMALL = V7X_LANES
GDN_Z = 4 * GDN_DIM + GDN_SMALL
HALO = V7X_SUBLANES
FFN_ROW_SLAB = 256
FFN_COL_SLAB = 512


def _params(sem, vmem_bytes=VMEM_LIMIT_BYTES):
    return pltpu.CompilerParams(dimension_semantics=sem,
                                vmem_limit_bytes=min(vmem_bytes, V7X_VMEM_BYTES))


def _nbytes(shape, dtype):
    return math.prod(shape) * jnp.dtype(dtype).itemsize


def _dot(a, b, precision=None):
    return jnp.dot(a, b, preferred_element_type=F32, precision=precision)


def _dot_nt(a, b):
    return lax.dot_general(a, b, (((1,), (1,)), ((), ())), preferred_element_type=F32)


def _silu(x):
    return x * jax.nn.sigmoid(x)


def _softplus(x):
    return jnp.maximum(x, 0.0) + jnp.log1p(jnp.exp(-jnp.abs(x)))


def _rms(x, g):
    return x * lax.rsqrt(jnp.mean(x * x, axis=-1, keepdims=True) + RMS_EPS) * g


def _ffn_kernel(*refs, nf, final):
    if final:
        x_ref, g_ref, wg_ref, wu_ref, wd_ref, fg_ref, o_ref, h_ref = refs
    else:
        x_ref, g_ref, wg_ref, wu_ref, wd_ref, o_ref, h_ref = refs
    j = pl.program_id(1)
    tm, d = o_ref.shape
    def for_row_slabs(fn):
        def body(r, carry):
            fn(pl.ds(pl.multiple_of(r * FFN_ROW_SLAB, FFN_ROW_SLAB), FFN_ROW_SLAB))
            return carry
        lax.fori_loop(0, tm // FFN_ROW_SLAB, body, 0)

    @pl.when(j == 0)
    def _():
        def norm(rs):
            h_ref[rs, :] = _rms(x_ref[rs, :], g_ref[...]).astype(BF16)
            o_ref[rs, :] = jnp.zeros((FFN_ROW_SLAB, d), F32)
        for_row_slabs(norm)

    h = h_ref[...]
    a = _dot(h, wg_ref[...])
    u = _dot(h, wu_ref[...])
    act = (_silu(a) * u).astype(BF16)
    for c in range(0, d, FFN_COL_SLAB):
        o_ref[:, c:c + FFN_COL_SLAB] += _dot(act, wd_ref[:, c:c + FFN_COL_SLAB])

    @pl.when(j == nf - 1)
    def _():
        def finish(rs):
            y = x_ref[rs, :] + 0.5 * o_ref[rs, :]
            if final:
                y = _rms(y, fg_ref[...])
            o_ref[rs, :] = y
        for_row_slabs(finish)


def _cast_kernel(x_ref, o_ref):
    o_ref[...] = x_ref[...].astype(o_ref.dtype)


def _to_bf16(w, *, block_bytes=4 * 1024 * 1024):
    l, r, c = w.shape
    rows = 1 << ((block_bytes // _nbytes((1, c), F32)).bit_length() - 1)
    assert rows >= 2 * V7X_SUBLANES and r % rows == 0, (w.shape, rows)
    spec = pl.BlockSpec((None, rows, c), lambda i, j: (i, j, 0))
    return pl.pallas_call(
        _cast_kernel,
        grid=(l, r // rows),
        in_specs=[spec],
        out_specs=spec,
        out_shape=jax.ShapeDtypeStruct(w.shape, BF16),
        compiler_params=_params(("parallel", "parallel"), 4 * (block_bytes + block_bytes // 2)),
        name="to_bf16",
    )(w)


def _ffn(x, g, wg, wu, wd, layer, final_g=None, *, tm=1024, tf=512):
    n, d = x.shape
    f = wg.shape[2]
    final = final_g is not None
    row = lambda i, j: (0, 0)
    in_specs = [
        pl.BlockSpec((tm, d), lambda i, j: (i, 0)),
        pl.BlockSpec((1, d), row),
        pl.BlockSpec((None, d, tf), lambda i, j: (layer, 0, j)),
        pl.BlockSpec((None, d, tf), lambda i, j: (layer, 0, j)),
        pl.BlockSpec((None, tf, d), lambda i, j: (layer, j, 0)),
    ]
    args = [x, g.reshape(1, d), wg, wu, wd]
    if final:
        in_specs.append(pl.BlockSpec((1, d), row))
        args.append(final_g.reshape(1, d))
    vmem = (2 * (2 * _nbytes((tm, d), F32) + 3 * _nbytes((d, tf), BF16)) + _nbytes((tm, d), BF16)
            + 2 * (2 * _nbytes((tm, tf), F32) + _nbytes((tm, tf), BF16) + _nbytes((tm, FFN_COL_SLAB), F32)))
    return pl.pallas_call(
        functools.partial(_ffn_kernel, nf=f // tf, final=final),
        grid=(n // tm, f // tf),
        in_specs=in_specs,
        out_specs=pl.BlockSpec((tm, d), lambda i, j: (i, 0)),
        out_shape=jax.ShapeDtypeStruct((n, d), F32),
        scratch_shapes=[pltpu.VMEM((tm, d), BF16)],
        compiler_params=_params(("parallel", "arbitrary"), vmem),
        name="ffn",
    )(*args)


def _norm_proj_kernel(x_ref, g_ref, *refs):
    w_refs, o_refs = refs[:len(refs) // 2], refs[len(refs) // 2:]
    h = _rms(x_ref[...], g_ref[...]).astype(BF16)
    for w_ref, o_ref in zip(w_refs, o_refs):
        o_ref[...] = _dot(h, w_ref[...])


def _norm_proj(x, g, ws, *, tm=512, name):
    n, d = x.shape
    cs = [w.shape[1] for w in ws]
    return pl.pallas_call(
        _norm_proj_kernel,
        grid=(n // tm,),
        in_specs=[pl.BlockSpec((tm, d), lambda i: (i, 0)), pl.BlockSpec((1, d), lambda i: (0, 0))]
                 + [pl.BlockSpec((d, c), lambda i: (0, 0), pipeline_mode=pl.Buffered(1)) for c in cs],
        out_specs=[pl.BlockSpec((tm, c), lambda i: (i, 0)) for c in cs],
        out_shape=[jax.ShapeDtypeStruct((n, c), F32) for c in cs],
        compiler_params=_params(("parallel",)),
        name=name,
    )(x, g.reshape(1, d), *ws)


def _out_proj_kernel(x_ref, yr_ref, yg_ref, yl_ref, wr_ref, wg_ref, wl_ref, o_ref):
    o_ref[...] = (x_ref[...] + _dot(yr_ref[...], wr_ref[...]) + _dot(yg_ref[...], wg_ref[...])
                  + _dot(yl_ref[...], wl_ref[...]))


def _out_proj(x, y_rw, y_gd, y_lr, w_rw, w_gd, w_lr, *, tm=512):
    n, d = x.shape
    tile = lambda c: pl.BlockSpec((tm, c), lambda i: (i, 0))
    whole = lambda c: pl.BlockSpec((c, d), lambda i: (0, 0), pipeline_mode=pl.Buffered(1))
    return pl.pallas_call(
        _out_proj_kernel,
        grid=(n // tm,),
        in_specs=[tile(d), tile(RWKV_DIM), tile(GDN_DIM), tile(LRU_DIM),
                  whole(RWKV_DIM), whole(GDN_DIM), whole(LRU_DIM)],
        out_specs=tile(d),
        out_shape=jax.ShapeDtypeStruct((n, d), F32),
        compiler_params=_params(("parallel",)),
        name="out_proj",
    )(x, y_rw, y_gd, y_lr, w_rw, w_gd, w_lr)


def _shift_rows(x, d, fill, row):
    return jnp.where(row >= d, pltpu.roll(x, d, 0), fill)


def _causal_taps(xh, taps):
    s = V7X_SUBLANES
    rows, c = xh.shape
    x3 = xh.reshape(rows // s, s, c)
    sub = lax.broadcasted_iota(jnp.int32, (rows // s - 1, s, c), 1)
    out = []
    for j in taps:
        rot = pltpu.roll(x3, j, 1)
        out.append(jnp.where(sub >= j, rot[1:], rot[:-1]).reshape(rows - s, c))
    return out


def _cumsum_rows(x, row):
    d = 1
    while d < x.shape[0]:
        x = x + _shift_rows(x, d, 0.0, row)
        d *= 2
    return x


def _block_diag(m, bd_mask):
    mb = m.astype(BF16)
    return jnp.where(bd_mask, jnp.concatenate([mb, mb], axis=0), jnp.zeros((), BF16))


def _pair_inverse_many(xs, eye2, bd_mask):
    c = CHUNK
    bd = lambda x: _block_diag(x, bd_mask)
    ss = [eye2 + x for x in xs]
    ps = [_dot(x.astype(BF16), bd(x)) for x in xs]
    for _ in range(c.bit_length() - 3):
        prods = [_dot(jnp.concatenate([p, s], axis=0).astype(BF16), bd(p)) for p, s in zip(ps, ss)]
        ps = [x[0:c] for x in prods]
        ss = [s + x[c:2 * c] for s, x in zip(ss, prods)]
    return [s + _dot(s.astype(BF16), bd(p)) for p, s in zip(ps, ss)]


def _seg_sum(x, ones_bd):
    hi = x.astype(BF16)
    lo = (x - hi.astype(F32)).astype(BF16)
    return _dot(hi, ones_bd) + _dot(lo, ones_bd)


def _pair_masks():
    c = CHUNK
    row = lax.broadcasted_iota(jnp.int32, (c, 2 * c), 0)
    lane = lax.broadcasted_iota(jnp.int32, (c, 2 * c), 1)
    col = jnp.where(lane >= c, lane - c, lane)
    r2 = lax.broadcasted_iota(jnp.int32, (2 * c, 2 * c), 0)
    l2 = lax.broadcasted_iota(jnp.int32, (2 * c, 2 * c), 1)
    bd_mask = (r2 >= c) == (l2 >= c)
    return dict(row=row, first=lane < c, strict=row > col, incl=row >= col,
                eye=(row == col).astype(F32), bd=bd_mask)


def _lru_kernel(z_ref, cw_ref, cb_ref, wa_ref, ba_ref, wx_ref, bx_ref, lam_ref, o_ref,
                xbuf_ref, h_ref, *, tt):
    @pl.when(pl.program_id(1) == 0)
    def _():
        xbuf_ref[0:HALO, :] = jnp.zeros((HALO, LRU_DIM), F32)
        h_ref[...] = jnp.zeros_like(h_ref)

    xl = z_ref[:, 0:LRU_DIM]
    yl = z_ref[:, LRU_DIM:2 * LRU_DIM]
    xbuf_ref[HALO:HALO + tt, :] = xl
    xc = cb_ref[...] + cw_ref[LRU_CONV - 1:LRU_CONV, :] * xl
    taps = _causal_taps(xbuf_ref[...], range(1, LRU_CONV))
    for j in range(1, LRU_CONV):
        xc = xc + cw_ref[LRU_CONV - 1 - j:LRU_CONV - j, :] * taps[j - 1]
    xbuf_ref[0:HALO, :] = xbuf_ref[tt:tt + HALO, :]

    xcb = xc.astype(BF16)
    half = LRU_DIM // 2
    ra = jnp.concatenate([_dot(xcb[:, p * half:(p + 1) * half], wa_ref[p]) for p in range(2)], axis=1)
    ia = jnp.concatenate([_dot(xcb[:, p * half:(p + 1) * half], wx_ref[p]) for p in range(2)], axis=1)
    r = jax.nn.sigmoid(ra + ba_ref[...])
    i = jax.nn.sigmoid(ia + bx_ref[...])
    log_a = -LRU_C * r * _softplus(-lam_ref[...])
    a = jnp.exp(log_a)
    u = jnp.sqrt(1.0 - a * a) * (i * xc)

    s = V7X_SUBLANES
    groups = tt // s
    a = a.reshape(groups, s, LRU_DIM)
    u = u.reshape(groups, s, LRU_DIM)
    sub = lax.broadcasted_iota(jnp.int32, (groups, s, LRU_DIM), 1)
    d = 1
    while d < s:
        u = a * jnp.where(sub >= d, pltpu.roll(u, d, 1), 0.0) + u
        a = a * jnp.where(sub >= d, pltpu.roll(a, d, 1), 1.0)
        d *= 2
    carry = h_ref[...]
    hs = []
    for g in range(groups):
        hs.append(u[g] + a[g] * carry)
        carry = hs[-1][s - 1:s, :]
    h_ref[...] = carry
    o_ref[...] = (jnp.concatenate(hs, axis=0) * jax.nn.gelu(yl)).astype(o_ref.dtype)


def _lru_mixer(z, conv_w, conv_b, wa_bd, b_a, wx_bd, b_x, lam, *, tt=512):
    b, t, _ = z.shape
    vec = lambda a: a.reshape(1, LRU_DIM)
    const2 = lambda shape: pl.BlockSpec(shape, lambda i, j: (0, 0))
    const3 = lambda shape: pl.BlockSpec(shape, lambda i, j: (0, 0, 0))
    half = LRU_DIM // 2
    return pl.pallas_call(
        functools.partial(_lru_kernel, tt=tt),
        grid=(b, t // tt),
        in_specs=[pl.BlockSpec((None, tt, LRU_IN), lambda i, j: (i, j, 0)),
                  const2((LRU_CONV, LRU_DIM)), const2((1, LRU_DIM)),
                  const3((2, half, half)), const2((1, LRU_DIM)),
                  const3((2, half, half)), const2((1, LRU_DIM)), const2((1, LRU_DIM))],
        out_specs=pl.BlockSpec((None, tt, LRU_DIM), lambda i, j: (i, j, 0)),
        out_shape=jax.ShapeDtypeStruct((b, t, LRU_DIM), BF16),
        scratch_shapes=[pltpu.VMEM((tt + HALO, LRU_DIM), F32), pltpu.VMEM((1, LRU_DIM), F32)],
        compiler_params=_params(("parallel", "arbitrary")),
        name="rglru",
    )(z, conv_w, vec(conv_b), wa_bd, vec(b_a), wx_bd, vec(b_x), vec(lam))


def _gdn_kernel(z_ref, cw_ref, alog_ref, dtb_ref, ng_ref, o_ref,
                xbuf_ref, q_ref, k_ref, v_ref, beta_ref, g_ref, s_ref, *, tt, group):
    c = CHUNK
    hd = GDN_HEAD_DIM
    qkv_dim = 3 * GDN_DIM

    @pl.when(pl.program_id(1) == 0)
    def _():
        xbuf_ref[0:HALO, :] = jnp.zeros((HALO, qkv_dim), F32)
        s_ref[...] = jnp.zeros_like(s_ref)

    xbuf_ref[HALO:HALO + tt, :] = z_ref[:, 0:qkv_dim]
    for s in range(3 * GDN_HEADS):
        sl = slice(s * hd, (s + 1) * hd)
        xh = xbuf_ref[:, sl]
        taps = _causal_taps(xh, range(1, GDN_CONV))
        acc = cw_ref[GDN_CONV - 1:GDN_CONV, sl] * xh[HALO:HALO + tt, :]
        for j in range(1, GDN_CONV):
            acc = acc + cw_ref[GDN_CONV - 1 - j:GDN_CONV - j, sl] * taps[j - 1]
        y = _silu(acc)
        which, head = divmod(s, GDN_HEADS)
        if which < 2:
            y = y * lax.rsqrt(jnp.sum(y * y, axis=-1, keepdims=True) + 1e-12)
        if which == 0:
            y = y * (hd ** -0.5)
        (q_ref, k_ref, v_ref)[which][:, head * hd:(head + 1) * hd] = y
    xbuf_ref[0:HALO, :] = xbuf_ref[tt:tt + HALO, :]

    small = z_ref[:, 4 * GDN_DIM:4 * GDN_DIM + GDN_SMALL]
    beta_ref[...] = jax.nn.sigmoid(small)
    g_ref[...] = -jnp.exp(alog_ref[...]) * _softplus(small + dtb_ref[...])

    m = _pair_masks()
    row_s = lax.broadcasted_iota(jnp.int32, (c, GDN_SMALL), 0)
    ng = ng_ref[...]

    def chunk_group(row_slices):
        nh = GDN_HEADS
        chunks = range(len(row_slices))
        hp = [(j, h) for j in chunks for h in range(nh)]
        pp = [(j, p) for j in chunks for p in range(nh // 2)]
        sls = [slice(h * hd, (h + 1) * hd) for h in range(nh)]
        sel = [m["first"], jnp.logical_not(m["first"])]
        gc = [_cumsum_rows(g_ref[rows, :], row_s) for rows in row_slices]
        gct = [x.T for x in gc]
        beta = [beta_ref[rows, :] for rows in row_slices]
        gcol = [gc[j][:, nh + h:nh + h + 1] for j, h in hp]
        grow = [gct[j][nh + h:nh + h + 1, :] for j, h in hp]
        bcol = [beta[j][:, h:h + 1] for j, h in hp]
        q = [q_ref[row_slices[j], sls[h]] for j, h in hp]
        k = [k_ref[row_slices[j], sls[h]] for j, h in hp]
        v = [v_ref[row_slices[j], sls[h]] for j, h in hp]
        n = range(len(hp))
        kb = [k[i] * bcol[i] for i in n]
        egc = [jnp.exp(gcol[i]) for i in n]
        zero = jnp.zeros((c, hd), BF16)
        kbf = [x.astype(BF16) for x in k]
        mm = []
        qk = []
        for j, p in pp:
            i0, i1 = j * nh + 2 * p, j * nh + 2 * p + 1
            gcol2 = jnp.where(m["first"], gcol[i0], gcol[i1])
            grow2 = jnp.concatenate([grow[i0], grow[i1]], axis=1)
            decay = jnp.where(m["incl"], jnp.exp(jnp.where(m["incl"], gcol2 - grow2, 0.0)), 0.0)
            lhs = jnp.concatenate([jnp.concatenate([kb[i0], kb[i1]], axis=1),
                                   jnp.concatenate([q[i0], q[i1]], axis=1)], axis=0).astype(BF16)
            rhs = jnp.concatenate([jnp.concatenate([kbf[i0], zero], axis=1),
                                   jnp.concatenate([zero, kbf[i1]], axis=1)], axis=0)
            kq = _dot_nt(lhs, rhs)
            mm.append(jnp.where(m["strict"], -(kq[0:c, :] * decay), 0.0))
            qk.append((kq[c:2 * c, :] * decay))
        qk2 = [jnp.concatenate([jnp.where(sel[0], x, 0.0), jnp.where(sel[1], x, 0.0)], axis=0).astype(BF16)
               for x in qk]
        t = _pair_inverse_many(mm, m["eye"], m["bd"])
        rhs_sol = [jnp.concatenate(
            [jnp.concatenate([v[i] * bcol[i], kb[i] * egc[i]], axis=1)
             for i in (j * nh + 2 * p, j * nh + 2 * p + 1)], axis=0).astype(BF16)
            for j, p in pp]
        sol = [_dot(jnp.where(sel[i % 2], t[i // 2], 0.0).astype(BF16), rhs_sol[i // 2])
               for i in n]
        qe = [(q[i] * egc[i]).astype(BF16) for i in n]
        g_last = [gcol[i][c - 1:c, :] for i in n]
        kd = [(k[i] * jnp.exp(g_last[i] - gcol[i])).T.astype(BF16) for i in n]
        s_mat = [s_ref[h] for h in range(nh)]
        for j in chunks:
            ids = [j * nh + h for h in range(nh)]
            s_b = [x.astype(BF16) for x in s_mat]
            ws = [_dot(jnp.concatenate([sol[i][:, hd:2 * hd].astype(BF16), qe[i]], axis=0), s_b[h])
                  for h, i in enumerate(ids)]
            vnb = [(sol[i][:, 0:hd] - ws[h][0:c]).astype(BF16) for h, i in enumerate(ids)]
            s_mat = [s_mat[h] * jnp.exp(g_last[i]) + _dot(kd[i], vnb[h]) for h, i in enumerate(ids)]
            qv = [_dot(qk2[j * (nh // 2) + p], jnp.concatenate([vnb[2 * p], vnb[2 * p + 1]], axis=0))
                  for p in range(nh // 2)]
            for h, i in enumerate(ids):
                oo = ws[h][c:2 * c] + qv[h // 2][(h % 2) * c:(h % 2 + 1) * c]
                gate = z_ref[row_slices[j], 3 * GDN_DIM + h * hd:3 * GDN_DIM + (h + 1) * hd]
                o_ref[row_slices[j], sls[h]] = (_rms(oo, ng) * _silu(gate)).astype(o_ref.dtype)
        for h in range(nh):
            s_ref[h] = s_mat[h]

    n_chunks = tt // c
    if group == n_chunks:
        chunk_group([slice(j * c, (j + 1) * c) for j in range(n_chunks)])
    else:
        def body(gi, carry):
            base = gi * (group * c)
            chunk_group([pl.ds(pl.multiple_of(base + j * c, c), c) for j in range(group)])
            return carry
        lax.fori_loop(0, n_chunks // group, body, 0)


def _gdn_mixer(z, conv_w, a_log_pad, dt_bias_pad, norm_g, *, tt=512, group=8):
    b, t, _ = z.shape
    const2 = lambda shape: pl.BlockSpec(shape, lambda i, j: (0, 0))
    return pl.pallas_call(
        functools.partial(_gdn_kernel, tt=tt, group=group),
        grid=(b, t // tt),
        in_specs=[pl.BlockSpec((None, tt, GDN_Z), lambda i, j: (i, j, 0)),
                  const2((GDN_CONV, 3 * GDN_DIM)), const2((1, GDN_SMALL)), const2((1, GDN_SMALL)),
                  const2((1, GDN_HEAD_DIM))],
        out_specs=pl.BlockSpec((None, tt, GDN_DIM), lambda i, j: (i, j, 0)),
        out_shape=jax.ShapeDtypeStruct((b, t, GDN_DIM), BF16),
        scratch_shapes=[pltpu.VMEM((tt + HALO, 3 * GDN_DIM), F32),
                        pltpu.VMEM((tt, GDN_DIM), F32), pltpu.VMEM((tt, GDN_DIM), F32),
                        pltpu.VMEM((tt, GDN_DIM), F32),
                        pltpu.VMEM((tt, GDN_SMALL), F32), pltpu.VMEM((tt, GDN_SMALL), F32),
                        pltpu.VMEM((GDN_HEADS, GDN_HEAD_DIM, GDN_HEAD_DIM), F32)],
        compiler_params=_params(("parallel", "arbitrary")),
        name="gdn",
    )(z, conv_w, a_log_pad, dt_bias_pad, norm_g.reshape(1, GDN_HEAD_DIM))


def _rwkv_kernel(z_ref, mu_ref, w0_ref, w2_ref, a0_ref, a2_ref, g2_ref, kk_ref, ka_ref, rk_ref,
                 lng_ref, lnb_ref, o_ref,
                 zbuf_ref, r_ref, k_ref, v_ref, n_ref, a_ref, lw_ref, y_ref, p_ref, *, tt, group):
    c = CHUNK
    w = V7X_LANES
    n_slabs = RWKV_DIM // w

    @pl.when(pl.program_id(1) == 0)
    def _():
        zbuf_ref[0:HALO, :] = jnp.zeros((HALO, RWKV_IN), F32)
        p_ref[...] = jnp.zeros_like(p_ref)

    m = _pair_masks()
    ones_bd = jnp.where(m["bd"], 1.0, 0.0).astype(BF16)
    not_first = jnp.logical_not(m["first"])

    zbuf_ref[HALO:HALO + tt, :] = z_ref[...]

    def shifted(lo, hi):
        zh = zbuf_ref[:, lo:hi]
        cur = zh[HALO:HALO + tt, :]
        prev, = _causal_taps(zh, (1,))
        return cur + mu_ref[:, lo:hi] * (prev - cur)

    lora = shifted(3 * RWKV_DIM, 3 * RWKV_DIM + DECAY_LORA + ICLR_LORA)
    lora_t = jnp.tanh(lora).astype(BF16)
    lora_b = lora.astype(BF16)
    gate_in = jax.nn.sigmoid(shifted(RWKV_IN - GATE_LORA, RWKV_IN)).astype(BF16)
    for s in range(n_slabs):
        sl = slice(s * w, (s + 1) * w)
        r = shifted(s * w, (s + 1) * w)
        k = shifted(RWKV_DIM + s * w, RWKV_DIM + (s + 1) * w)
        v = shifted(2 * RWKV_DIM + s * w, 2 * RWKV_DIM + (s + 1) * w)
        lw = -RWKV_DECAY_SCALE * jax.nn.sigmoid(w0_ref[:, sl] + _dot(lora_t, w2_ref[:, sl]))
        a = jax.nn.sigmoid(a0_ref[:, sl] + _dot(lora_b, a2_ref[:, sl]))
        kn = k * kk_ref[:, sl]
        kn = kn * lax.rsqrt(_seg_sum(kn * kn, ones_bd) + 1e-12)
        r_ref[:, sl] = r
        k_ref[:, sl] = k * (1.0 + (a - 1.0) * ka_ref[:, sl])
        v_ref[:, sl] = v
        n_ref[:, sl] = kn
        a_ref[:, sl] = a
        lw_ref[:, sl] = lw
    zbuf_ref[0:HALO, :] = zbuf_ref[tt:tt + HALO, :]

    def chunk_group(row_slices):
        slabs = range(n_slabs)
        sls = [slice(s * w, (s + 1) * w) for s in slabs]
        probs = [(rows, sl) for rows in row_slices for sl in sls]
        n = range(len(probs))
        bd = lambda x: _block_diag(x, m["bd"])
        r = [r_ref[rows, sl] for rows, sl in probs]
        k = [k_ref[rows, sl] for rows, sl in probs]
        v = [v_ref[rows, sl] for rows, sl in probs]
        kn = [n_ref[rows, sl] for rows, sl in probs]
        a = [a_ref[rows, sl] for rows, sl in probs]
        lw = [lw_ref[rows, sl] for rows, sl in probs]
        cum = [_cumsum_rows(x, m["row"]) for x in lw]
        b = [kn[i] * a[i] for i in n]
        at = [-kn[i] * jnp.exp(cum[i] - lw[i]) for i in n]
        rt = [(r[i] * jnp.exp(cum[i])).astype(BF16) for i in n]
        e_out = [jnp.exp(-x) for x in cum]
        bt = [b[i] * e_out[i] for i in n]
        kt = [k[i] * e_out[i] for i in n]
        aa = []
        for i in n:
            lhs = jnp.concatenate([at[i].astype(BF16), rt[i]], axis=0)
            rhs = jnp.concatenate(
                [jnp.where(m["first"], bt[i], 0.0), jnp.where(not_first, bt[i], 0.0),
                 jnp.where(m["first"], kt[i], 0.0), jnp.where(not_first, kt[i], 0.0)],
                axis=0).astype(BF16)
            aa.append(_dot_nt(lhs, rhs))
        a_ab = [jnp.where(m["strict"], x[0:c, 0:w], 0.0) for x in aa]
        a_ak = [jnp.where(m["strict"], x[0:c, w:2 * w], 0.0).astype(BF16) for x in aa]
        a_rb = [jnp.where(m["incl"], x[c:2 * c, 0:w], 0.0).astype(BF16) for x in aa]
        a_rk = [jnp.where(m["incl"], x[c:2 * c, w:2 * w], 0.0).astype(BF16) for x in aa]
        v_bd = [bd(x) for x in v]
        av = [_dot(jnp.concatenate([a_ak[i], a_rk[i]], axis=0), v_bd[i]) for i in n]
        t = _pair_inverse_many(a_ab, m["eye"], m["bd"])
        tu = [_dot(t[i].astype(BF16), jnp.concatenate([bd(av[i][0:c]), bd(at[i])], axis=1))
              for i in n]
        lhs_p = [jnp.concatenate([tu[i][:, w:2 * w].astype(BF16), rt[i]], axis=0) for i in n]
        kb_t = []
        d_col = []
        for i in n:
            d_end = jnp.exp(cum[i][c - 1:c, :]) * e_out[i]
            kb_t.append(jnp.concatenate([k[i] * d_end, b[i] * d_end], axis=0).T.astype(BF16))
            d_col.append(jnp.exp(cum[i].T[:, c - 1:c]))
        p_mat = [p_ref[s] for s in slabs]
        for j, rows in enumerate(row_slices):
            ids = [j * n_slabs + s for s in slabs]
            p_b = [x.astype(BF16) for x in p_mat]
            sp = [_dot(lhs_p[i], p_b[s]) for s, i in enumerate(ids)]
            u = [tu[i][:, 0:w] + sp[s][0:c] for s, i in enumerate(ids)]
            upd = [_dot(kb_t[i], jnp.concatenate([v[i], u[s]], axis=0).astype(BF16))
                   for s, i in enumerate(ids)]
            p_mat = [d_col[i] * p_mat[s] + jnp.where(m["bd"], upd[s], 0.0) for s, i in enumerate(ids)]
            for s, i in enumerate(ids):
                y_ref[rows, sls[s]] = av[i][c:2 * c] + sp[s][c:2 * c] + _dot(a_rb[i], bd(u[s]))
        for s in slabs:
            p_ref[s] = p_mat[s]

    n_chunks = tt // c
    if group == n_chunks:
        chunk_group([slice(j * c, (j + 1) * c) for j in range(n_chunks)])
    else:
        def body(gi, carry):
            base = gi * (group * c)
            chunk_group([pl.ds(pl.multiple_of(base + j * c, c), c) for j in range(group)])
            return carry
        lax.fori_loop(0, n_chunks // group, body, 0)

    inv_n = 1.0 / RWKV_HEAD_DIM
    for s in range(n_slabs):
        sl = slice(s * w, (s + 1) * w)
        y = y_ref[:, sl]
        mean = _seg_sum(y, ones_bd) * inv_n
        yc = y - mean
        var = _seg_sum(yc * yc, ones_bd) * inv_n
        y = yc * lax.rsqrt(var + RWKV_GN_EPS) * lng_ref[:, sl] + lnb_ref[:, sl]
        bonus = _seg_sum(r_ref[:, sl] * k_ref[:, sl] * rk_ref[:, sl], ones_bd) * v_ref[:, sl]
        g = _dot(gate_in, g2_ref[:, sl])
        o_ref[:, sl] = ((y + bonus) * g).astype(o_ref.dtype)


def _rwkv_mixer(z, mu, w0, w2_pad, a0, a2_pad, g2, k_k, k_a, r_k, ln_g, ln_b, *, tt=512, group=8):
    b, t, _ = z.shape
    lora = DECAY_LORA + ICLR_LORA
    const2 = lambda shape: pl.BlockSpec(shape, lambda i, j: (0, 0))
    vec = lambda a: a.reshape(1, RWKV_DIM)
    tile = pltpu.VMEM((tt, RWKV_DIM), F32)
    return pl.pallas_call(
        functools.partial(_rwkv_kernel, tt=tt, group=group),
        grid=(b, t // tt),
        in_specs=[pl.BlockSpec((None, tt, RWKV_IN), lambda i, j: (i, j, 0)),
                  const2((1, RWKV_IN)), const2((1, RWKV_DIM)), const2((lora, RWKV_DIM)),
                  const2((1, RWKV_DIM)), const2((lora, RWKV_DIM)), const2((GATE_LORA, RWKV_DIM))]
                 + [const2((1, RWKV_DIM))] * 5,
        out_specs=pl.BlockSpec((None, tt, RWKV_DIM), lambda i, j: (i, j, 0)),
        out_shape=jax.ShapeDtypeStruct((b, t, RWKV_DIM), BF16),
        scratch_shapes=[pltpu.VMEM((tt + HALO, RWKV_IN), F32)] + [tile] * 7
                       + [pltpu.VMEM((RWKV_DIM // V7X_LANES, V7X_LANES, V7X_LANES), F32)],
        compiler_params=_params(("parallel", "arbitrary")),
        name="rwkv7",
    )(z, mu.reshape(1, RWKV_IN), vec(w0), w2_pad, vec(a0), a2_pad, g2, vec(k_k), vec(k_a),
      vec(r_k), vec(ln_g), vec(ln_b))


def _block_diag_weights(w):
    per = (LRU_DIM // 2) // LRU_BLOCK_DIM
    out = jnp.zeros((2, LRU_DIM // 2, LRU_DIM // 2), F32)
    for blk in range(LRU_BLOCKS):
        p, i = divmod(blk, per)
        lo = i * LRU_BLOCK_DIM
        out = out.at[p, lo:lo + LRU_BLOCK_DIM, lo:lo + LRU_BLOCK_DIM].set(w[blk])
    return out.astype(BF16)


def kernel(x, norm1_g, ffn1_wg, ffn1_wu, ffn1_wd, norm_mix_g, w_in, rw_mu, rw_w0, rw_w2, rw_a0, rw_a2, rw_g2, rw_kk, rw_ka, rw_rk, rw_ln_g, rw_ln_b, gd_conv_w, gd_a_log, gd_dt_bias, gd_norm_g, lr_conv_w, lr_conv_b, lr_wa, lr_ba, lr_wx, lr_bx, lr_lam, w_out, norm2_g, ffn2_wg, ffn2_wu, ffn2_wd, final_g):
    b, t, d = x.shape
    n = b * t
    bf = lambda a: a.astype(BF16)
    xs = x.reshape(n, d)
    ffn1 = [_to_bf16(w) for w in (ffn1_wg, ffn1_wu, ffn1_wd)]
    ffn2 = [_to_bf16(w) for w in (ffn2_wg, ffn2_wu, ffn2_wd)]
    for l in range(DEPTH):
        xs = _ffn(xs, norm1_g[l], *ffn1, l)

        w_l = w_in[l]
        w_rw = bf(w_l[:, :RWKV_IN])
        w_gd = bf(jnp.pad(w_l[:, RWKV_IN:RWKV_IN + GDN_IN], ((0, 0), (0, GDN_Z - GDN_IN))))
        w_lr = bf(w_l[:, RWKV_IN + GDN_IN:])
        z_rw, z_lr = _norm_proj(xs, norm_mix_g[l], [w_rw, w_lr], name="in_proj_rwkv_lru")
        z_gd, = _norm_proj(xs, norm_mix_g[l], [w_gd], name="in_proj_gdn")
        z_rw = z_rw.reshape(b, t, RWKV_IN)
        z_gd = z_gd.reshape(b, t, GDN_Z)
        z_lr = z_lr.reshape(b, t, LRU_IN)

        w2_pad = bf(jnp.concatenate([rw_w2[l], jnp.zeros((ICLR_LORA, RWKV_DIM), F32)], axis=0))
        a2_pad = bf(jnp.concatenate([jnp.zeros((DECAY_LORA, RWKV_DIM), F32), rw_a2[l]], axis=0))
        y_rw = _rwkv_mixer(z_rw, rw_mu[l], rw_w0[l], w2_pad, rw_a0[l], a2_pad, bf(rw_g2[l]),
                           rw_kk[l], rw_ka[l], rw_rk[l].reshape(RWKV_DIM), rw_ln_g[l], rw_ln_b[l])

        lane_pad = lambda a: jnp.pad(a, (GDN_HEADS, GDN_SMALL - 2 * GDN_HEADS)).reshape(1, GDN_SMALL)
        y_gd = _gdn_mixer(z_gd, gd_conv_w[l], lane_pad(gd_a_log[l]), lane_pad(gd_dt_bias[l]),
                          gd_norm_g[l])

        y_lr = _lru_mixer(z_lr, lr_conv_w[l], lr_conv_b[l], _block_diag_weights(lr_wa[l]), lr_ba[l],
                          _block_diag_weights(lr_wx[l]), lr_bx[l], lr_lam[l])

        w_o = bf(w_out[l])
        xs = _out_proj(xs, y_rw.reshape(n, RWKV_DIM), y_gd.reshape(n, GDN_DIM),
                       y_lr.reshape(n, LRU_DIM), w_o[:RWKV_DIM], w_o[RWKV_DIM:RWKV_DIM + GDN_DIM],
                       w_o[RWKV_DIM + GDN_DIM:])
        last = l == DEPTH - 1
        xs = _ffn(xs, norm2_g[l], *ffn2, l, final_g if last else None)
    return xs.reshape(b, t, d)
```

```python
import functools
import math

import jax
import jax.numpy as jnp
from jax import lax
from jax.experimental import pallas as pl
from jax.experimental.pallas import tpu as pltpu

F32 = jnp.float32
BF16 = jnp.bfloat16

D_MODEL = 2048
DEPTH = 2
CHUNK = 64
RMS_EPS = 1e-6
D_FF = 5632
RWKV_HEADS = 12
RWKV_HEAD_DIM = 64
RWKV_DIM = RWKV_HEADS * RWKV_HEAD_DIM
DECAY_LORA = 64
ICLR_LORA = 64
GATE_LORA = 128
RWKV_GN_EPS = 64e-5
RWKV_DECAY_SCALE = math.exp(-0.5)
GDN_HEADS = 6
GDN_HEAD_DIM = 128
GDN_DIM = GDN_HEADS * GDN_HEAD_DIM
GDN_CONV = 4
LRU_BLOCKS = 8
LRU_BLOCK_DIM = 64
LRU_DIM = LRU_BLOCKS * LRU_BLOCK_DIM
LRU_CONV = 4
LRU_C = 8.0
RWKV_IN = 3 * RWKV_DIM + DECAY_LORA + ICLR_LORA + GATE_LORA
GDN_IN = 4 * GDN_DIM + 2 * GDN_HEADS
LRU_IN = 2 * LRU_DIM

V7X_LANES = 128
V7X_SUBLANES = 8
V7X_VMEM_BYTES = 64 * 1024 * 1024
VMEM_LIMIT_BYTES = V7X_VMEM_BYTES - 8 * 1024 * 1024

GDN_SMALL = V7X_LANES
GDN_Z = 4 * GDN_DIM + GDN_SMALL
HALO = V7X_SUBLANES
FFN_ROW_SLAB = 256
FFN_COL_SLAB = 512


def _params(sem, vmem_bytes=VMEM_LIMIT_BYTES):
    return pltpu.CompilerParams(dimension_semantics=sem,
                                vmem_limit_bytes=min(vmem_bytes, V7X_VMEM_BYTES))


def _nbytes(shape, dtype):
    return math.prod(shape) * jnp.dtype(dtype).itemsize


def _dot(a, b, precision=None):
    return jnp.dot(a, b, preferred_element_type=F32, precision=precision)


def _dot_nt(a, b):
    return lax.dot_general(a, b, (((1,), (1,)), ((), ())), preferred_element_type=F32)


def _silu(x):
    return x * jax.nn.sigmoid(x)


def _softplus(x):
    return jnp.maximum(x, 0.0) + jnp.log1p(jnp.exp(-jnp.abs(x)))


def _rms(x, g):
    return x * lax.rsqrt(jnp.mean(x * x, axis=-1, keepdims=True) + RMS_EPS) * g


def _ffn_kernel(*refs, nf, final):
    if final:
        x_ref, g_ref, wg_ref, wu_ref, wd_ref, fg_ref, o_ref, h_ref = refs
    else:
        x_ref, g_ref, wg_ref, wu_ref, wd_ref, o_ref, h_ref = refs
    j = pl.program_id(1)
    tm, d = o_ref.shape
    def for_row_slabs(fn):
        def body(r, carry):
            fn(pl.ds(pl.multiple_of(r * FFN_ROW_SLAB, FFN_ROW_SLAB), FFN_ROW_SLAB))
            return carry
        lax.fori_loop(0, tm // FFN_ROW_SLAB, body, 0)

    @pl.when(j == 0)
    def _():
        def norm(rs):
            h_ref[rs, :] = _rms(x_ref[rs, :], g_ref[...]).astype(BF16)
            o_ref[rs, :] = jnp.zeros((FFN_ROW_SLAB, d), F32)
        for_row_slabs(norm)

    h = h_ref[...]
    a = _dot(h, wg_ref[...])
    u = _dot(h, wu_ref[...])
    act = (_silu(a) * u).astype(BF16)
    for c in range(0, d, FFN_COL_SLAB):
        o_ref[:, c:c + FFN_COL_SLAB] += _dot(act, wd_ref[:, c:c + FFN_COL_SLAB])

    @pl.when(j == nf - 1)
    def _():
        def finish(rs):
            y = x_ref[rs, :] + 0.5 * o_ref[rs, :]
            if final:
                y = _rms(y, fg_ref[...])
            o_ref[rs, :] = y
        for_row_slabs(finish)


def _cast_kernel(x_ref, o_ref):
    o_ref[...] = x_ref[...].astype(o_ref.dtype)


def _to_bf16(w, *, block_bytes=4 * 1024 * 1024):
    l, r, c = w.shape
    rows = 1 << ((block_bytes // _nbytes((1, c), F32)).bit_length() - 1)
    assert rows >= 2 * V7X_SUBLANES and r % rows == 0, (w.shape, rows)
    spec = pl.BlockSpec((None, rows, c), lambda i, j: (i, j, 0))
    return pl.pallas_call(
        _cast_kernel,
        grid=(l, r // rows),
        in_specs=[spec],
        out_specs=spec,
        out_shape=jax.ShapeDtypeStruct(w.shape, BF16),
        compiler_params=_params(("parallel", "parallel"), 4 * (block_bytes + block_bytes // 2)),
        name="to_bf16",
    )(w)


def _ffn(x, g, wg, wu, wd, layer, final_g=None, *, tm=1024, tf=512):
    n, d = x.shape
    f = wg.shape[2]
    final = final_g is not None
    row = lambda i, j: (0, 0)
    in_specs = [
        pl.BlockSpec((tm, d), lambda i, j: (i, 0)),
        pl.BlockSpec((1, d), row),
        pl.BlockSpec((None, d, tf), lambda i, j: (layer, 0, j)),
        pl.BlockSpec((None, d, tf), lambda i, j: (layer, 0, j)),
        pl.BlockSpec((None, tf, d), lambda i, j: (layer, j, 0)),
    ]
    args = [x, g.reshape(1, d), wg, wu, wd]
    if final:
        in_specs.append(pl.BlockSpec((1, d), row))
        args.append(final_g.reshape(1, d))
    vmem = (2 * (2 * _nbytes((tm, d), F32) + 3 * _nbytes((d, tf), BF16)) + _nbytes((tm, d), BF16)
            + 2 * (2 * _nbytes((tm, tf), F32) + _nbytes((tm, tf), BF16) + _nbytes((tm, FFN_COL_SLAB), F32)))
    return pl.pallas_call(
        functools.partial(_ffn_kernel, nf=f // tf, final=final),
        grid=(n // tm, f // tf),
        in_specs=in_specs,
        out_specs=pl.BlockSpec((tm, d), lambda i, j: (i, 0)),
        out_shape=jax.ShapeDtypeStruct((n, d), F32),
        scratch_shapes=[pltpu.VMEM((tm, d), BF16)],
        compiler_params=_params(("parallel", "arbitrary"), vmem),
        name="ffn",
    )(*args)


def _norm_proj_kernel(x_ref, g_ref, *refs):
    w_refs, o_refs = refs[:len(refs) // 2], refs[len(refs) // 2:]
    h = _rms(x_ref[...], g_ref[...]).astype(BF16)
    for w_ref, o_ref in zip(w_refs, o_refs):
        o_ref[...] = _dot(h, w_ref[...])


def _norm_proj(x, g, ws, *, tm=512, name):
    n, d = x.shape
    cs = [w.shape[1] for w in ws]
    return pl.pallas_call(
        _norm_proj_kernel,
        grid=(n // tm,),
        in_specs=[pl.BlockSpec((tm, d), lambda i: (i, 0)), pl.BlockSpec((1, d), lambda i: (0, 0))]
                 + [pl.BlockSpec((d, c), lambda i: (0, 0), pipeline_mode=pl.Buffered(1)) for c in cs],
        out_specs=[pl.BlockSpec((tm, c), lambda i: (i, 0)) for c in cs],
        out_shape=[jax.ShapeDtypeStruct((n, c), F32) for c in cs],
        compiler_params=_params(("parallel",)),
        name=name,
    )(x, g.reshape(1, d), *ws)


def _out_proj_kernel(x_ref, yr_ref, yg_ref, yl_ref, wr_ref, wg_ref, wl_ref, o_ref):
    o_ref[...] = (x_ref[...] + _dot(yr_ref[...], wr_ref[...]) + _dot(yg_ref[...], wg_ref[...])
                  + _dot(yl_ref[...], wl_ref[...]))


def _out_proj(x, y_rw, y_gd, y_lr, w_rw, w_gd, w_lr, *, tm=512):
    n, d = x.shape
    tile = lambda c: pl.BlockSpec((tm, c), lambda i: (i, 0))
    whole = lambda c: pl.BlockSpec((c, d), lambda i: (0, 0), pipeline_mode=pl.Buffered(1))
    return pl.pallas_call(
        _out_proj_kernel,
        grid=(n // tm,),
        in_specs=[tile(d), tile(RWKV_DIM), tile(GDN_DIM), tile(LRU_DIM),
                  whole(RWKV_DIM), whole(GDN_DIM), whole(LRU_DIM)],
        out_specs=tile(d),
        out_shape=jax.ShapeDtypeStruct((n, d), F32),
        compiler_params=_params(("parallel",)),
        name="out_proj",
    )(x, y_rw, y_gd, y_lr, w_rw, w_gd, w_lr)


def _causal_taps(xh, taps):
    s = V7X_SUBLANES
    rows, c = xh.shape
    x3 = xh.reshape(rows // s, s, c)
    sub = lax.broadcasted_iota(jnp.int32, (rows // s - 1, s, c), 1)
    out = []
    for j in taps:
        rot = pltpu.roll(x3, j, 1)
        out.append(jnp.where(sub >= j, rot[1:], rot[:-1]).reshape(rows - s, c))
    return out


def _cumsum_rows(x):
    s = V7X_SUBLANES
    rows, c = x.shape
    x3 = x.reshape(rows // s, s, c)
    sub = lax.broadcasted_iota(jnp.int32, x3.shape, 1)
    d = 1
    while d < s:
        x3 = x3 + jnp.where(sub >= d, pltpu.roll(x3, d, 1), 0.0)
        d *= 2
    out = [x3[0]]
    for i in range(1, rows // s):
        out.append(x3[i] + out[-1][s - 1:s, :])
    return jnp.concatenate(out, axis=0)


def _block_diag(m, bd_mask):
    mb = m.astype(BF16)
    return jnp.where(bd_mask, jnp.concatenate([mb, mb], axis=0), jnp.zeros((), BF16))


def _pair_inverse_many(xs, eye2, bd_mask):
    c = CHUNK
    bd = lambda x: _block_diag(x, bd_mask)
    ss = [eye2 + x for x in xs]
    ps = [_dot(x.astype(BF16), bd(x)) for x in xs]
    for _ in range(c.bit_length() - 3):
        prods = [_dot(jnp.concatenate([p, s], axis=0).astype(BF16), bd(p)) for p, s in zip(ps, ss)]
        ps = [x[0:c] for x in prods]
        ss = [s + x[c:2 * c] for s, x in zip(ss, prods)]
    return [s + _dot(s.astype(BF16), bd(p)) for p, s in zip(ps, ss)]


def _seg_sum(x, first):
    s0 = jnp.sum(jnp.where(first, x, 0.0), axis=-1, keepdims=True)
    s1 = jnp.sum(jnp.where(first, 0.0, x), axis=-1, keepdims=True)
    return jnp.where(first, s0, s1)


def _pair_masks():
    c = CHUNK
    row = lax.broadcasted_iota(jnp.int32, (c, 2 * c), 0)
    lane = lax.broadcasted_iota(jnp.int32, (c, 2 * c), 1)
    col = jnp.where(lane >= c, lane - c, lane)
    r2 = lax.broadcasted_iota(jnp.int32, (2 * c, 2 * c), 0)
    l2 = lax.broadcasted_iota(jnp.int32, (2 * c, 2 * c), 1)
    bd_mask = (r2 >= c) == (l2 >= c)
    return dict(first=lane < c, strict=row > col, incl=row >= col,
                eye=(row == col).astype(F32), bd=bd_mask)


def _lru_kernel(z_ref, cw_ref, cb_ref, wa_ref, ba_ref, wx_ref, bx_ref, lam_ref, o_ref,
                xbuf_ref, h_ref, *, tt):
    @pl.when(pl.program_id(1) == 0)
    def _():
        xbuf_ref[0:HALO, :] = jnp.zeros((HALO, LRU_DIM), F32)
        h_ref[...] = jnp.zeros_like(h_ref)

    xl = z_ref[:, 0:LRU_DIM]
    yl = z_ref[:, LRU_DIM:2 * LRU_DIM]
    xbuf_ref[HALO:HALO + tt, :] = xl
    xc = cb_ref[...] + cw_ref[LRU_CONV - 1:LRU_CONV, :] * xl
    taps = _causal_taps(xbuf_ref[...], range(1, LRU_CONV))
    for j in range(1, LRU_CONV):
        xc = xc + cw_ref[LRU_CONV - 1 - j:LRU_CONV - j, :] * taps[j - 1]
    xbuf_ref[0:HALO, :] = xbuf_ref[tt:tt + HALO, :]

    xcb = xc.astype(BF16)
    half = LRU_DIM // 2
    ra = jnp.concatenate([_dot(xcb[:, p * half:(p + 1) * half], wa_ref[p]) for p in range(2)], axis=1)
    ia = jnp.concatenate([_dot(xcb[:, p * half:(p + 1) * half], wx_ref[p]) for p in range(2)], axis=1)
    r = jax.nn.sigmoid(ra + ba_ref[...])
    i = jax.nn.sigmoid(ia + bx_ref[...])
    log_a = -LRU_C * r * _softplus(-lam_ref[...])
    a = jnp.exp(log_a)
    u = jnp.sqrt(1.0 - a * a) * (i * xc)

    s = V7X_SUBLANES
    groups = tt // s
    a = a.reshape(groups, s, LRU_DIM)
    u = u.reshape(groups, s, LRU_DIM)
    sub = lax.broadcasted_iota(jnp.int32, (groups, s, LRU_DIM), 1)
    d = 1
    while d < s:
        u = a * jnp.where(sub >= d, pltpu.roll(u, d, 1), 0.0) + u
        a = a * jnp.where(sub >= d, pltpu.roll(a, d, 1), 1.0)
        d *= 2
    carry = h_ref[...]
    hs = []
    for g in range(groups):
        hs.append(u[g] + a[g] * carry)
        carry = hs[-1][s - 1:s, :]
    h_ref[...] = carry
    o_ref[...] = (jnp.concatenate(hs, axis=0) * jax.nn.gelu(yl)).astype(o_ref.dtype)


def _lru_mixer(z, conv_w, conv_b, wa_bd, b_a, wx_bd, b_x, lam, *, tt=512):
    b, t, _ = z.shape
    vec = lambda a: a.reshape(1, LRU_DIM)
    const2 = lambda shape: pl.BlockSpec(shape, lambda i, j: (0, 0))
    const3 = lambda shape: pl.BlockSpec(shape, lambda i, j: (0, 0, 0))
    half = LRU_DIM // 2
    return pl.pallas_call(
        functools.partial(_lru_kernel, tt=tt),
        grid=(b, t // tt),
        in_specs=[pl.BlockSpec((None, tt, LRU_IN), lambda i, j: (i, j, 0)),
                  const2((LRU_CONV, LRU_DIM)), const2((1, LRU_DIM)),
                  const3((2, half, half)), const2((1, LRU_DIM)),
                  const3((2, half, half)), const2((1, LRU_DIM)), const2((1, LRU_DIM))],
        out_specs=pl.BlockSpec((None, tt, LRU_DIM), lambda i, j: (i, j, 0)),
        out_shape=jax.ShapeDtypeStruct((b, t, LRU_DIM), BF16),
        scratch_shapes=[pltpu.VMEM((tt + HALO, LRU_DIM), F32), pltpu.VMEM((1, LRU_DIM), F32)],
        compiler_params=_params(("parallel", "arbitrary")),
        name="rglru",
    )(z, conv_w, vec(conv_b), wa_bd, vec(b_a), wx_bd, vec(b_x), vec(lam))


def _gdn_kernel(z_ref, cw_ref, alog_ref, dtb_ref, ng_ref, o_ref,
                xbuf_ref, q_ref, k_ref, v_ref, beta_ref, g_ref, s_ref, *, tt):
    c = CHUNK
    hd = GDN_HEAD_DIM
    qkv_dim = 3 * GDN_DIM

    @pl.when(pl.program_id(1) == 0)
    def _():
        xbuf_ref[0:HALO, :] = jnp.zeros((HALO, qkv_dim), F32)
        s_ref[...] = jnp.zeros_like(s_ref)

    xbuf_ref[HALO:HALO + tt, :] = z_ref[:, 0:qkv_dim]
    for s in range(3 * GDN_HEADS):
        sl = slice(s * hd, (s + 1) * hd)
        xh = xbuf_ref[:, sl]
        taps = _causal_taps(xh, range(1, GDN_CONV))
        acc = cw_ref[GDN_CONV - 1:GDN_CONV, sl] * xh[HALO:HALO + tt, :]
        for j in range(1, GDN_CONV):
            acc = acc + cw_ref[GDN_CONV - 1 - j:GDN_CONV - j, sl] * taps[j - 1]
        y = _silu(acc)
        which, head = divmod(s, GDN_HEADS)
        if which < 2:
            y = y * lax.rsqrt(jnp.sum(y * y, axis=-1, keepdims=True) + 1e-12)
        if which == 0:
            y = y * (hd ** -0.5)
        (q_ref, k_ref, v_ref)[which][:, head * hd:(head + 1) * hd] = y
    xbuf_ref[0:HALO, :] = xbuf_ref[tt:tt + HALO, :]

    small = z_ref[:, 4 * GDN_DIM:4 * GDN_DIM + GDN_SMALL]
    beta_ref[...] = jax.nn.sigmoid(small)
    g_ref[...] = -jnp.exp(alog_ref[...]) * _softplus(small + dtb_ref[...])

    m = _pair_masks()
    ng = ng_ref[...]

    def chunk_group(row_slices):
        nh = GDN_HEADS
        chunks = range(len(row_slices))
        hp = [(j, h) for j in chunks for h in range(nh)]
        pp = [(j, p) for j in chunks for p in range(nh // 2)]
        sls = [slice(h * hd, (h + 1) * hd) for h in range(nh)]
        sel = [m["first"], jnp.logical_not(m["first"])]
        gc = [_cumsum_rows(g_ref[rows, :]) for rows in row_slices]
        gct = [x.T for x in gc]
        beta = [beta_ref[rows, :] for rows in row_slices]
        gcol = [gc[j][:, nh + h:nh + h + 1] for j, h in hp]
        grow = [gct[j][nh + h:nh + h + 1, :] for j, h in hp]
        bcol = [beta[j][:, h:h + 1] for j, h in hp]
        q = [q_ref[row_slices[j], sls[h]] for j, h in hp]
        k = [k_ref[row_slices[j], sls[h]] for j, h in hp]
        v = [v_ref[row_slices[j], sls[h]] for j, h in hp]
        n = range(len(hp))
        kb = [k[i] * bcol[i] for i in n]
        egc = [jnp.exp(gcol[i]) for i in n]
        zero = jnp.zeros((c, hd), BF16)
        kbf = [x.astype(BF16) for x in k]
        mm = []
        qk = []
        for j, p in pp:
            i0, i1 = j * nh + 2 * p, j * nh + 2 * p + 1
            gcol2 = jnp.where(m["first"], gcol[i0], gcol[i1])
            grow2 = jnp.concatenate([grow[i0], grow[i1]], axis=1)
            decay = jnp.where(m["incl"], jnp.exp(jnp.where(m["incl"], gcol2 - grow2, 0.0)), 0.0)
            lhs = jnp.concatenate([jnp.concatenate([kb[i0], kb[i1]], axis=1),
                                   jnp.concatenate([q[i0], q[i1]], axis=1)], axis=0).astype(BF16)
            rhs = jnp.concatenate([jnp.concatenate([kbf[i0], zero], axis=1),
                                   jnp.concatenate([zero, kbf[i1]], axis=1)], axis=0)
            kq = _dot_nt(lhs, rhs)
            mm.append(jnp.where(m["strict"], -(kq[0:c, :] * decay), 0.0))
            qk.append((kq[c:2 * c, :] * decay))
        qk2 = [jnp.concatenate([jnp.where(sel[0], x, 0.0), jnp.where(sel[1], x, 0.0)], axis=0).astype(BF16)
               for x in qk]
        t = _pair_inverse_many(mm, m["eye"], m["bd"])
        rhs_sol = [jnp.concatenate(
            [jnp.concatenate([v[i] * bcol[i], kb[i] * egc[i]], axis=1)
             for i in (j * nh + 2 * p, j * nh + 2 * p + 1)], axis=0).astype(BF16)
            for j, p in pp]
        sol = [_dot(jnp.where(sel[i % 2], t[i // 2], 0.0).astype(BF16), rhs_sol[i // 2])
               for i in n]
        qe = [(q[i] * egc[i]).astype(BF16) for i in n]
        g_last = [gcol[i][c - 1:c, :] for i in n]
        kd = [(k[i] * jnp.exp(g_last[i] - gcol[i])).T.astype(BF16) for i in n]
        s_mat = [s_ref[h] for h in range(nh)]
        for j in chunks:
            ids = [j * nh + h for h in range(nh)]
            s_b = [x.astype(BF16) for x in s_mat]
            ws = [_dot(jnp.concatenate([sol[i][:, hd:2 * hd].astype(BF16), qe[i]], axis=0), s_b[h])
                  for h, i in enumerate(ids)]
            vnb = [(sol[i][:, 0:hd] - ws[h][0:c]).astype(BF16) for h, i in enumerate(ids)]
            s_mat = [s_mat[h] * jnp.exp(g_last[i]) + _dot(kd[i], vnb[h]) for h, i in enumerate(ids)]
            qv = [_dot(qk2[j * (nh // 2) + p], jnp.concatenate([vnb[2 * p], vnb[2 * p + 1]], axis=0))
                  for p in range(nh // 2)]
            for h, i in enumerate(ids):
                oo = ws[h][c:2 * c] + qv[h // 2][(h % 2) * c:(h % 2 + 1) * c]
                gate = z_ref[row_slices[j], 3 * GDN_DIM + h * hd:3 * GDN_DIM + (h + 1) * hd]
                o_ref[row_slices[j], sls[h]] = (_rms(oo, ng) * _silu(gate)).astype(o_ref.dtype)
        for h in range(nh):
            s_ref[h] = s_mat[h]

    chunk_group([slice(j * c, (j + 1) * c) for j in range(tt // c)])


def _gdn_mixer(z, conv_w, a_log_pad, dt_bias_pad, norm_g, *, tt=512):
    b, t, _ = z.shape
    const2 = lambda shape: pl.BlockSpec(shape, lambda i, j: (0, 0))
    return pl.pallas_call(
        functools.partial(_gdn_kernel, tt=tt),
        grid=(b, t // tt),
        in_specs=[pl.BlockSpec((None, tt, GDN_Z), lambda i, j: (i, j, 0)),
                  const2((GDN_CONV, 3 * GDN_DIM)), const2((1, GDN_SMALL)), const2((1, GDN_SMALL)),
                  const2((1, GDN_HEAD_DIM))],
        out_specs=pl.BlockSpec((None, tt, GDN_DIM), lambda i, j: (i, j, 0)),
        out_shape=jax.ShapeDtypeStruct((b, t, GDN_DIM), BF16),
        scratch_shapes=[pltpu.VMEM((tt + HALO, 3 * GDN_DIM), F32),
                        pltpu.VMEM((tt, GDN_DIM), F32), pltpu.VMEM((tt, GDN_DIM), F32),
                        pltpu.VMEM((tt, GDN_DIM), F32),
                        pltpu.VMEM((tt, GDN_SMALL), F32), pltpu.VMEM((tt, GDN_SMALL), F32),
                        pltpu.VMEM((GDN_HEADS, GDN_HEAD_DIM, GDN_HEAD_DIM), F32)],
        compiler_params=_params(("parallel", "arbitrary")),
        name="gdn",
    )(z, conv_w, a_log_pad, dt_bias_pad, norm_g.reshape(1, GDN_HEAD_DIM))


def _rwkv_kernel(z_ref, mu_ref, w0_ref, w2_ref, a0_ref, a2_ref, g2_ref, kk_ref, ka_ref, rk_ref,
                 lng_ref, lnb_ref, o_ref,
                 zbuf_ref, r_ref, k_ref, v_ref, n_ref, a_ref, lw_ref, y_ref, p_ref, *, tt):
    c = CHUNK
    w = V7X_LANES
    n_slabs = RWKV_DIM // w

    @pl.when(pl.program_id(1) == 0)
    def _():
        zbuf_ref[0:HALO, :] = jnp.zeros((HALO, RWKV_IN), F32)
        p_ref[...] = jnp.zeros_like(p_ref)

    m = _pair_masks()
    first_head = lax.broadcasted_iota(jnp.int32, (1, w), 1) < RWKV_HEAD_DIM
    not_first = jnp.logical_not(m["first"])

    zbuf_ref[HALO:HALO + tt, :] = z_ref[...]

    def shifted(lo, hi):
        zh = zbuf_ref[:, lo:hi]
        cur = zh[HALO:HALO + tt, :]
        prev, = _causal_taps(zh, (1,))
        return cur + mu_ref[:, lo:hi] * (prev - cur)

    lora = shifted(3 * RWKV_DIM, 3 * RWKV_DIM + DECAY_LORA + ICLR_LORA)
    lora_t = jnp.tanh(lora).astype(BF16)
    lora_b = lora.astype(BF16)
    gate_in = jax.nn.sigmoid(shifted(RWKV_IN - GATE_LORA, RWKV_IN)).astype(BF16)
    for s in range(n_slabs):
        sl = slice(s * w, (s + 1) * w)
        r = shifted(s * w, (s + 1) * w)
        k = shifted(RWKV_DIM + s * w, RWKV_DIM + (s + 1) * w)
        v = shifted(2 * RWKV_DIM + s * w, 2 * RWKV_DIM + (s + 1) * w)
        lw = -RWKV_DECAY_SCALE * jax.nn.sigmoid(w0_ref[:, sl] + _dot(lora_t, w2_ref[:, sl]))
        a = jax.nn.sigmoid(a0_ref[:, sl] + _dot(lora_b, a2_ref[:, sl]))
        kn = k * kk_ref[:, sl]
        kn = kn * lax.rsqrt(_seg_sum(kn * kn, first_head) + 1e-12)
        r_ref[:, sl] = r
        k_ref[:, sl] = k * (1.0 + (a - 1.0) * ka_ref[:, sl])
        v_ref[:, sl] = v
        n_ref[:, sl] = kn
        a_ref[:, sl] = a
        lw_ref[:, sl] = lw
    zbuf_ref[0:HALO, :] = zbuf_ref[tt:tt + HALO, :]

    slabs = range(n_slabs)
    sls = [slice(s * w, (s + 1) * w) for s in slabs]
    bd = lambda x: _block_diag(x, m["bd"])

    def independent_part(probs):
        n = range(len(probs))
        r = [r_ref[rows, sl] for rows, sl in probs]
        k = [k_ref[rows, sl] for rows, sl in probs]
        v = [v_ref[rows, sl] for rows, sl in probs]
        kn = [n_ref[rows, sl] for rows, sl in probs]
        a = [a_ref[rows, sl] for rows, sl in probs]
        lw = [lw_ref[rows, sl] for rows, sl in probs]
        cum = [_cumsum_rows(x) for x in lw]
        b = [kn[i] * a[i] for i in n]
        at = [-kn[i] * jnp.exp(cum[i] - lw[i]) for i in n]
        rt = [(r[i] * jnp.exp(cum[i])).astype(BF16) for i in n]
        e_out = [jnp.exp(-x) for x in cum]
        bt = [b[i] * e_out[i] for i in n]
        kt = [k[i] * e_out[i] for i in n]
        aa = []
        for i in n:
            lhs = jnp.concatenate([at[i].astype(BF16), rt[i]], axis=0)
            rhs = jnp.concatenate(
                [jnp.where(m["first"], bt[i], 0.0), jnp.where(not_first, bt[i], 0.0),
                 jnp.where(m["first"], kt[i], 0.0), jnp.where(not_first, kt[i], 0.0)],
                axis=0).astype(BF16)
            aa.append(_dot_nt(lhs, rhs))
        a_ab = [jnp.where(m["strict"], x[0:c, 0:w], 0.0) for x in aa]
        a_ak = [jnp.where(m["strict"], x[0:c, w:2 * w], 0.0).astype(BF16) for x in aa]
        a_rb = [jnp.where(m["incl"], x[c:2 * c, 0:w], 0.0).astype(BF16) for x in aa]
        a_rk = [jnp.where(m["incl"], x[c:2 * c, w:2 * w], 0.0).astype(BF16) for x in aa]
        v_bd = [bd(x) for x in v]
        av = [_dot(jnp.concatenate([a_ak[i], a_rk[i]], axis=0), v_bd[i]) for i in n]
        t = _pair_inverse_many(a_ab, m["eye"], m["bd"])
        tu = [_dot(t[i].astype(BF16), jnp.concatenate([bd(av[i][0:c]), bd(at[i])], axis=1))
              for i in n]
        lhs_p = [jnp.concatenate([tu[i][:, w:2 * w].astype(BF16), rt[i]], axis=0) for i in n]
        kb_t = []
        d_col = []
        for i in n:
            d_end = jnp.exp(cum[i][c - 1:c, :]) * e_out[i]
            kb_t.append(jnp.concatenate([k[i] * d_end, b[i] * d_end], axis=0).T.astype(BF16))
            d_col.append(jnp.exp(cum[i].T[:, c - 1:c]))
        return dict(v=v, u0=[x[:, 0:w] for x in tu], lhs_p=lhs_p, kb_t=kb_t, d_col=d_col,
                    y0=[x[c:2 * c] for x in av], a_rb=a_rb)

    row_slices = [slice(j * c, (j + 1) * c) for j in range(tt // c)]
    q = independent_part([(rows, sl) for rows in row_slices for sl in sls])

    p_mat = [p_ref[s] for s in slabs]
    for j, rows in enumerate(row_slices):
        ids = [j * n_slabs + s for s in slabs]
        p_b = [x.astype(BF16) for x in p_mat]
        sp = [_dot(q["lhs_p"][i], p_b[s]) for s, i in enumerate(ids)]
        u = [q["u0"][i] + sp[s][0:c] for s, i in enumerate(ids)]
        upd = [_dot(q["kb_t"][i], jnp.concatenate([q["v"][i], u[s]], axis=0).astype(BF16))
               for s, i in enumerate(ids)]
        p_mat = [q["d_col"][i] * p_mat[s] + jnp.where(m["bd"], upd[s], 0.0) for s, i in enumerate(ids)]
        for s, i in enumerate(ids):
            y_ref[rows, sls[s]] = q["y0"][i] + sp[s][c:2 * c] + _dot(q["a_rb"][i], bd(u[s]))
    for s in slabs:
        p_ref[s] = p_mat[s]

    inv_n = 1.0 / RWKV_HEAD_DIM
    for s in range(n_slabs):
        sl = slice(s * w, (s + 1) * w)
        y = y_ref[:, sl]
        mean = _seg_sum(y, first_head) * inv_n
        yc = y - mean
        var = _seg_sum(yc * yc, first_head) * inv_n
        y = yc * lax.rsqrt(var + RWKV_GN_EPS) * lng_ref[:, sl] + lnb_ref[:, sl]
        bonus = _seg_sum(r_ref[:, sl] * k_ref[:, sl] * rk_ref[:, sl], first_head) * v_ref[:, sl]
        g = _dot(gate_in, g2_ref[:, sl])
        o_ref[:, sl] = ((y + bonus) * g).astype(o_ref.dtype)


def _rwkv_mixer(z, mu, w0, w2_pad, a0, a2_pad, g2, k_k, k_a, r_k, ln_g, ln_b, *, tt=512):
    b, t, _ = z.shape
    lora = DECAY_LORA + ICLR_LORA
    const2 = lambda shape: pl.BlockSpec(shape, lambda i, j: (0, 0))
    vec = lambda a: a.reshape(1, RWKV_DIM)
    tile = pltpu.VMEM((tt, RWKV_DIM), F32)
    return pl.pallas_call(
        functools.partial(_rwkv_kernel, tt=tt),
        grid=(b, t // tt),
        in_specs=[pl.BlockSpec((None, tt, RWKV_IN), lambda i, j: (i, j, 0)),
                  const2((1, RWKV_IN)), const2((1, RWKV_DIM)), const2((lora, RWKV_DIM)),
                  const2((1, RWKV_DIM)), const2((lora, RWKV_DIM)), const2((GATE_LORA, RWKV_DIM))]
                 + [const2((1, RWKV_DIM))] * 5,
        out_specs=pl.BlockSpec((None, tt, RWKV_DIM), lambda i, j: (i, j, 0)),
        out_shape=jax.ShapeDtypeStruct((b, t, RWKV_DIM), BF16),
        scratch_shapes=[pltpu.VMEM((tt + HALO, RWKV_IN), F32)] + [tile] * 7
                       + [pltpu.VMEM((RWKV_DIM // V7X_LANES, V7X_LANES, V7X_LANES), F32)],
        compiler_params=_params(("parallel", "arbitrary")),
        name="rwkv7",
    )(z, mu.reshape(1, RWKV_IN), vec(w0), w2_pad, vec(a0), a2_pad, g2, vec(k_k), vec(k_a),
      vec(r_k), vec(ln_g), vec(ln_b))


def _block_diag_weights(w):
    per = (LRU_DIM // 2) // LRU_BLOCK_DIM
    out = jnp.zeros((2, LRU_DIM // 2, LRU_DIM // 2), F32)
    for blk in range(LRU_BLOCKS):
        p, i = divmod(blk, per)
        lo = i * LRU_BLOCK_DIM
        out = out.at[p, lo:lo + LRU_BLOCK_DIM, lo:lo + LRU_BLOCK_DIM].set(w[blk])
    return out.astype(BF16)


def kernel(x, norm1_g, ffn1_wg, ffn1_wu, ffn1_wd, norm_mix_g, w_in, rw_mu, rw_w0, rw_w2, rw_a0, rw_a2, rw_g2, rw_kk, rw_ka, rw_rk, rw_ln_g, rw_ln_b, gd_conv_w, gd_a_log, gd_dt_bias, gd_norm_g, lr_conv_w, lr_conv_b, lr_wa, lr_ba, lr_wx, lr_bx, lr_lam, w_out, norm2_g, ffn2_wg, ffn2_wu, ffn2_wd, final_g):
    b, t, d = x.shape
    n = b * t
    bf = lambda a: a.astype(BF16)
    xs = x.reshape(n, d)
    ffn1 = [_to_bf16(w) for w in (ffn1_wg, ffn1_wu, ffn1_wd)]
    ffn2 = [_to_bf16(w) for w in (ffn2_wg, ffn2_wu, ffn2_wd)]
    for l in range(DEPTH):
        xs = _ffn(xs, norm1_g[l], *ffn1, l)

        w_l = w_in[l]
        w_rw = bf(w_l[:, :RWKV_IN])
        w_gd = bf(jnp.pad(w_l[:, RWKV_IN:RWKV_IN + GDN_IN], ((0, 0), (0, GDN_Z - GDN_IN))))
        w_lr = bf(w_l[:, RWKV_IN + GDN_IN:])
        z_rw, z_lr = _norm_proj(xs, norm_mix_g[l], [w_rw, w_lr], name="in_proj_rwkv_lru")
        z_gd, = _norm_proj(xs, norm_mix_g[l], [w_gd], name="in_proj_gdn")
        z_rw = z_rw.reshape(b, t, RWKV_IN)
        z_gd = z_gd.reshape(b, t, GDN_Z)
        z_lr = z_lr.reshape(b, t, LRU_IN)

        w2_pad = bf(jnp.concatenate([rw_w2[l], jnp.zeros((ICLR_LORA, RWKV_DIM), F32)], axis=0))
        a2_pad = bf(jnp.concatenate([jnp.zeros((DECAY_LORA, RWKV_DIM), F32), rw_a2[l]], axis=0))
        y_rw = _rwkv_mixer(z_rw, rw_mu[l], rw_w0[l], w2_pad, rw_a0[l], a2_pad, bf(rw_g2[l]),
                           rw_kk[l], rw_ka[l], rw_rk[l].reshape(RWKV_DIM), rw_ln_g[l], rw_ln_b[l])

        lane_pad = lambda a: jnp.pad(a, (GDN_HEADS, GDN_SMALL - 2 * GDN_HEADS)).reshape(1, GDN_SMALL)
        y_gd = _gdn_mixer(z_gd, gd_conv_w[l], lane_pad(gd_a_log[l]), lane_pad(gd_dt_bias[l]),
                          gd_norm_g[l])

        y_lr = _lru_mixer(z_lr, lr_conv_w[l], lr_conv_b[l], _block_diag_weights(lr_wa[l]), lr_ba[l],
                          _block_diag_weights(lr_wx[l]), lr_bx[l], lr_lam[l])

        w_o = bf(w_out[l])
        xs = _out_proj(xs, y_rw.reshape(n, RWKV_DIM), y_gd.reshape(n, GDN_DIM),
                       y_lr.reshape(n, LRU_DIM), w_o[:RWKV_DIM], w_o[RWKV_DIM:RWKV_DIM + GDN_DIM],
                       w_o[RWKV_DIM + GDN_DIM:])
        last = l == DEPTH - 1
        xs = _ffn(xs, norm2_g[l], *ffn2, l, final_g if last else None)
    return xs.reshape(b, t, d)
```

```python
import functools
import math

import jax
import jax.numpy as jnp
from jax import lax
from jax.experimental import pallas as pl
from jax.experimental.pallas import tpu as pltpu

F32 = jnp.float32
BF16 = jnp.bfloat16

D_MODEL = 2048
DEPTH = 2
CHUNK = 64
RMS_EPS = 1e-6
D_FF = 5632
RWKV_HEADS = 12
RWKV_HEAD_DIM = 64
RWKV_DIM = RWKV_HEADS * RWKV_HEAD_DIM
DECAY_LORA = 64
ICLR_LORA = 64
GATE_LORA = 128
RWKV_GN_EPS = 64e-5
RWKV_DECAY_SCALE = math.exp(-0.5)
GDN_HEADS = 6
GDN_HEAD_DIM = 128
GDN_DIM = GDN_HEADS * GDN_HEAD_DIM
GDN_CONV = 4
LRU_BLOCKS = 8
LRU_BLOCK_DIM = 64
LRU_DIM = LRU_BLOCKS * LRU_BLOCK_DIM
LRU_CONV = 4
LRU_C = 8.0
RWKV_IN = 3 * RWKV_DIM + DECAY_LORA + ICLR_LORA + GATE_LORA
GDN_IN = 4 * GDN_DIM + 2 * GDN_HEADS
LRU_IN = 2 * LRU_DIM

V7X_LANES = 128
V7X_SUBLANES = 8
V7X_VMEM_BYTES = 64 * 1024 * 1024
VMEM_LIMIT_BYTES = V7X_VMEM_BYTES - 8 * 1024 * 1024

GDN_SMALL = V7X_LANES
GDN_Z = 4 * GDN_DIM + GDN_SMALL
HALO = V7X_SUBLANES
FFN_ROW_SLAB = 256
FFN_COL_SLAB = 512


def _params(sem, vmem_bytes=VMEM_LIMIT_BYTES):
    return pltpu.CompilerParams(dimension_semantics=sem,
                                vmem_limit_bytes=min(vmem_bytes, V7X_VMEM_BYTES))


def _nbytes(shape, dtype):
    return math.prod(shape) * jnp.dtype(dtype).itemsize


def _dot(a, b, precision=None):
    return jnp.dot(a, b, preferred_element_type=F32, precision=precision)


def _dot_nt(a, b):
    return lax.dot_general(a, b, (((1,), (1,)), ((), ())), preferred_element_type=F32)


def _silu(x):
    return x * jax.nn.sigmoid(x)


def _softplus(x):
    return jnp.maximum(x, 0.0) + jnp.log1p(jnp.exp(-jnp.abs(x)))


def _rms(x, g):
    return x * lax.rsqrt(jnp.mean(x * x, axis=-1, keepdims=True) + RMS_EPS) * g


def _ffn_kernel(*refs, nf, final):
    if final:
        x_ref, g_ref, wg_ref, wu_ref, wd_ref, fg_ref, o_ref, h_ref = refs
    else:
        x_ref, g_ref, wg_ref, wu_ref, wd_ref, o_ref, h_ref = refs
    j = pl.program_id(1)
    tm, d = o_ref.shape
    def for_row_slabs(fn):
        def body(r, carry):
            fn(pl.ds(pl.multiple_of(r * FFN_ROW_SLAB, FFN_ROW_SLAB), FFN_ROW_SLAB))
            return carry
        lax.fori_loop(0, tm // FFN_ROW_SLAB, body, 0)

    @pl.when(j == 0)
    def _():
        def norm(rs):
            h_ref[rs, :] = _rms(x_ref[rs, :], g_ref[...]).astype(BF16)
            o_ref[rs, :] = jnp.zeros((FFN_ROW_SLAB, d), F32)
        for_row_slabs(norm)

    h = h_ref[...]
    a = _dot(h, wg_ref[...])
    u = _dot(h, wu_ref[...])
    act = (_silu(a) * u).astype(BF16)
    for c in range(0, d, FFN_COL_SLAB):
        o_ref[:, c:c + FFN_COL_SLAB] += _dot(act, wd_ref[:, c:c + FFN_COL_SLAB])

    @pl.when(j == nf - 1)
    def _():
        def finish(rs):
            y = x_ref[rs, :] + 0.5 * o_ref[rs, :]
            if final:
                y = _rms(y, fg_ref[...])
            o_ref[rs, :] = y
        for_row_slabs(finish)


def _cast_kernel(x_ref, o_ref):
    o_ref[...] = x_ref[...].astype(o_ref.dtype)


def _to_bf16(w, *, block_bytes=4 * 1024 * 1024):
    l, r, c = w.shape
    rows = 1 << ((block_bytes // _nbytes((1, c), F32)).bit_length() - 1)
    assert rows >= 2 * V7X_SUBLANES and r % rows == 0, (w.shape, rows)
    spec = pl.BlockSpec((None, rows, c), lambda i, j: (i, j, 0))
    return pl.pallas_call(
        _cast_kernel,
        grid=(l, r // rows),
        in_specs=[spec],
        out_specs=spec,
        out_shape=jax.ShapeDtypeStruct(w.shape, BF16),
        compiler_params=_params(("parallel", "parallel"), 4 * (block_bytes + block_bytes // 2)),
        name="to_bf16",
    )(w)


def _ffn(x, g, wg, wu, wd, layer, final_g=None, *, tm=1024, tf=512):
    n, d = x.shape
    f = wg.shape[2]
    final = final_g is not None
    row = lambda i, j: (0, 0)
    in_specs = [
        pl.BlockSpec((tm, d), lambda i, j: (i, 0)),
        pl.BlockSpec((1, d), row),
        pl.BlockSpec((None, d, tf), lambda i, j: (layer, 0, j)),
        pl.BlockSpec((None, d, tf), lambda i, j: (layer, 0, j)),
        pl.BlockSpec((None, tf, d), lambda i, j: (layer, j, 0)),
    ]
    args = [x, g.reshape(1, d), wg, wu, wd]
    if final:
        in_specs.append(pl.BlockSpec((1, d), row))
        args.append(final_g.reshape(1, d))
    vmem = (2 * (2 * _nbytes((tm, d), F32) + 3 * _nbytes((d, tf), BF16)) + _nbytes((tm, d), BF16)
            + 2 * (2 * _nbytes((tm, tf), F32) + _nbytes((tm, tf), BF16) + _nbytes((tm, FFN_COL_SLAB), F32)))
    return pl.pallas_call(
        functools.partial(_ffn_kernel, nf=f // tf, final=final),
        grid=(n // tm, f // tf),
        in_specs=in_specs,
        out_specs=pl.BlockSpec((tm, d), lambda i, j: (i, 0)),
        out_shape=jax.ShapeDtypeStruct((n, d), F32),
        scratch_shapes=[pltpu.VMEM((tm, d), BF16)],
        compiler_params=_params(("parallel", "arbitrary"), vmem),
        name="ffn",
    )(*args)


def _norm_proj_kernel(x_ref, g_ref, *refs):
    w_refs, o_refs = refs[:len(refs) // 2], refs[len(refs) // 2:]
    h = _rms(x_ref[...], g_ref[...]).astype(BF16)
    for w_ref, o_ref in zip(w_refs, o_refs):
        o_ref[...] = _dot(h, w_ref[...])


def _norm_proj(x, g, ws, *, tm=512, name):
    n, d = x.shape
    cs = [w.shape[1] for w in ws]
    return pl.pallas_call(
        _norm_proj_kernel,
        grid=(n // tm,),
        in_specs=[pl.BlockSpec((tm, d), lambda i: (i, 0)), pl.BlockSpec((1, d), lambda i: (0, 0))]
                 + [pl.BlockSpec((d, c), lambda i: (0, 0), pipeline_mode=pl.Buffered(1)) for c in cs],
        out_specs=[pl.BlockSpec((tm, c), lambda i: (i, 0)) for c in cs],
        out_shape=[jax.ShapeDtypeStruct((n, c), F32) for c in cs],
        compiler_params=_params(("parallel",)),
        name=name,
    )(x, g.reshape(1, d), *ws)


def _out_proj_kernel(x_ref, yr_ref, yg_ref, yl_ref, wr_ref, wg_ref, wl_ref, o_ref):
    o_ref[...] = (x_ref[...] + _dot(yr_ref[...], wr_ref[...]) + _dot(yg_ref[...], wg_ref[...])
                  + _dot(yl_ref[...], wl_ref[...]))


def _out_proj(x, y_rw, y_gd, y_lr, w_rw, w_gd, w_lr, *, tm=512):
    n, d = x.shape
    tile = lambda c: pl.BlockSpec((tm, c), lambda i: (i, 0))
    whole = lambda c: pl.BlockSpec((c, d), lambda i: (0, 0), pipeline_mode=pl.Buffered(1))
    return pl.pallas_call(
        _out_proj_kernel,
        grid=(n // tm,),
        in_specs=[tile(d), tile(RWKV_DIM), tile(GDN_DIM), tile(LRU_DIM),
                  whole(RWKV_DIM), whole(GDN_DIM), whole(LRU_DIM)],
        out_specs=tile(d),
        out_shape=jax.ShapeDtypeStruct((n, d), F32),
        compiler_params=_params(("parallel",)),
        name="out_proj",
    )(x, y_rw, y_gd, y_lr, w_rw, w_gd, w_lr)


def _causal_taps(xh, taps):
    s = V7X_SUBLANES
    rows, c = xh.shape
    x3 = xh.reshape(rows // s, s, c)
    sub = lax.broadcasted_iota(jnp.int32, (rows // s - 1, s, c), 1)
    out = []
    for j in taps:
        rot = pltpu.roll(x3, j, 1)
        out.append(jnp.where(sub >= j, rot[1:], rot[:-1]).reshape(rows - s, c))
    return out


def _cumsum_rows(x):
    s = V7X_SUBLANES
    rows, c = x.shape
    x3 = x.reshape(rows // s, s, c)
    sub = lax.broadcasted_iota(jnp.int32, x3.shape, 1)
    d = 1
    while d < s:
        x3 = x3 + jnp.where(sub >= d, pltpu.roll(x3, d, 1), 0.0)
        d *= 2
    out = [x3[0]]
    for i in range(1, rows // s):
        out.append(x3[i] + out[-1][s - 1:s, :])
    return jnp.concatenate(out, axis=0)


def _block_diag(m, bd_mask):
    mb = m.astype(BF16)
    return jnp.where(bd_mask, jnp.concatenate([mb, mb], axis=0), jnp.zeros((), BF16))


def _pair_inverse_many(xs, eye2, bd_mask):
    c = CHUNK
    bd = lambda x: _block_diag(x, bd_mask)
    ss = [eye2 + x for x in xs]
    ps = [_dot(x.astype(BF16), bd(x)) for x in xs]
    for _ in range(c.bit_length() - 3):
        prods = [_dot(jnp.concatenate([p, s], axis=0).astype(BF16), bd(p)) for p, s in zip(ps, ss)]
        ps = [x[0:c] for x in prods]
        ss = [s + x[c:2 * c] for s, x in zip(ss, prods)]
    return [s + _dot(s.astype(BF16), bd(p)) for p, s in zip(ps, ss)]


def _seg_sum(x, first):
    s0 = jnp.sum(jnp.where(first, x, 0.0), axis=-1, keepdims=True)
    s1 = jnp.sum(jnp.where(first, 0.0, x), axis=-1, keepdims=True)
    return jnp.where(first, s0, s1)


def _pair_masks():
    c = CHUNK
    row = lax.broadcasted_iota(jnp.int32, (c, 2 * c), 0)
    lane = lax.broadcasted_iota(jnp.int32, (c, 2 * c), 1)
    col = jnp.where(lane >= c, lane - c, lane)
    r2 = lax.broadcasted_iota(jnp.int32, (2 * c, 2 * c), 0)
    l2 = lax.broadcasted_iota(jnp.int32, (2 * c, 2 * c), 1)
    bd_mask = (r2 >= c) == (l2 >= c)
    return dict(first=lane < c, strict=row > col, incl=row >= col,
                eye=(row == col).astype(F32), bd=bd_mask)


def _lru_tile(xl, yl, first, cw_ref, cb_ref, wa_ref, ba_ref, wx_ref, bx_ref, lam_ref, o_ref,
              xbuf_ref, h_ref, *, tt):
    @pl.when(first)
    def _():
        xbuf_ref[0:HALO, :] = jnp.zeros((HALO, LRU_DIM), F32)
        h_ref[...] = jnp.zeros_like(h_ref)

    xbuf_ref[HALO:HALO + tt, :] = xl
    xc = cb_ref[...] + cw_ref[LRU_CONV - 1:LRU_CONV, :] * xl
    taps = _causal_taps(xbuf_ref[...], range(1, LRU_CONV))
    for j in range(1, LRU_CONV):
        xc = xc + cw_ref[LRU_CONV - 1 - j:LRU_CONV - j, :] * taps[j - 1]
    xbuf_ref[0:HALO, :] = xbuf_ref[tt:tt + HALO, :]

    xcb = xc.astype(BF16)
    half = LRU_DIM // 2
    ra = jnp.concatenate([_dot(xcb[:, p * half:(p + 1) * half], wa_ref[p]) for p in range(2)], axis=1)
    ia = jnp.concatenate([_dot(xcb[:, p * half:(p + 1) * half], wx_ref[p]) for p in range(2)], axis=1)
    r = jax.nn.sigmoid(ra + ba_ref[...])
    i = jax.nn.sigmoid(ia + bx_ref[...])
    log_a = -LRU_C * r * _softplus(-lam_ref[...])
    a = jnp.exp(log_a)
    u = jnp.sqrt(1.0 - a * a) * (i * xc)

    s = V7X_SUBLANES
    groups = tt // s
    a = a.reshape(groups, s, LRU_DIM)
    u = u.reshape(groups, s, LRU_DIM)
    sub = lax.broadcasted_iota(jnp.int32, (groups, s, LRU_DIM), 1)
    d = 1
    while d < s:
        u = a * jnp.where(sub >= d, pltpu.roll(u, d, 1), 0.0) + u
        a = a * jnp.where(sub >= d, pltpu.roll(a, d, 1), 1.0)
        d *= 2
    carry = h_ref[...]
    hs = []
    for g in range(groups):
        hs.append(u[g] + a[g] * carry)
        carry = hs[-1][s - 1:s, :]
    h_ref[...] = carry
    o_ref[...] = (jnp.concatenate(hs, axis=0) * jax.nn.gelu(yl)).astype(o_ref.dtype)


def _proj_lru_kernel(x_ref, g_ref, wrw_ref, wlr_ref, cw_ref, cb_ref, wa_ref, ba_ref, wx_ref, bx_ref,
                     lam_ref, zrw_ref, ylr_ref, xbuf_ref, h_ref, *, tt, tiles_per_seq):
    h = _rms(x_ref[...], g_ref[...]).astype(BF16)
    z_lr = _dot(h, wlr_ref[...])
    first = pl.program_id(0) % tiles_per_seq == 0
    _lru_tile(z_lr[:, 0:LRU_DIM], z_lr[:, LRU_DIM:2 * LRU_DIM], first, cw_ref, cb_ref, wa_ref, ba_ref,
              wx_ref, bx_ref, lam_ref, ylr_ref, xbuf_ref, h_ref, tt=tt)
    zrw_ref[...] = _dot(h, wrw_ref[...])


def _proj_rwkv_lru(x, g, w_rw, w_lr, conv_w, conv_b, wa_bd, b_a, wx_bd, b_x, lam, *, seq_len, tt=512):
    n, d = x.shape
    vec = lambda a: a.reshape(1, LRU_DIM)
    const2 = lambda shape: pl.BlockSpec(shape, lambda i: (0, 0))
    resident = lambda shape: pl.BlockSpec(shape, lambda i: (0, 0), pipeline_mode=pl.Buffered(1))
    const3 = lambda shape: pl.BlockSpec(shape, lambda i: (0, 0, 0))
    half = LRU_DIM // 2
    tile = lambda c: pl.BlockSpec((tt, c), lambda i: (i, 0))
    return pl.pallas_call(
        functools.partial(_proj_lru_kernel, tt=tt, tiles_per_seq=seq_len // tt),
        grid=(n // tt,),
        in_specs=[tile(d), const2((1, d)), resident((d, RWKV_IN)), resident((d, LRU_IN)),
                  const2((LRU_CONV, LRU_DIM)), const2((1, LRU_DIM)),
                  const3((2, half, half)), const2((1, LRU_DIM)),
                  const3((2, half, half)), const2((1, LRU_DIM)), const2((1, LRU_DIM))],
        out_specs=[tile(RWKV_IN), tile(LRU_DIM)],
        out_shape=[jax.ShapeDtypeStruct((n, RWKV_IN), F32), jax.ShapeDtypeStruct((n, LRU_DIM), BF16)],
        scratch_shapes=[pltpu.VMEM((tt + HALO, LRU_DIM), F32), pltpu.VMEM((1, LRU_DIM), F32)],
        compiler_params=_params(("arbitrary",)),
        name="in_proj_rwkv_rglru",
    )(x, g.reshape(1, d), w_rw, w_lr, conv_w, vec(conv_b), wa_bd, vec(b_a), wx_bd, vec(b_x), vec(lam))


def _gdn_kernel(z_ref, cw_ref, alog_ref, dtb_ref, ng_ref, o_ref,
                xbuf_ref, q_ref, k_ref, v_ref, beta_ref, g_ref, s_ref, *, tt):
    c = CHUNK
    hd = GDN_HEAD_DIM
    qkv_dim = 3 * GDN_DIM

    @pl.when(pl.program_id(1) == 0)
    def _():
        xbuf_ref[0:HALO, :] = jnp.zeros((HALO, qkv_dim), F32)
        s_ref[...] = jnp.zeros_like(s_ref)

    xbuf_ref[HALO:HALO + tt, :] = z_ref[:, 0:qkv_dim]
    for s in range(3 * GDN_HEADS):
        sl = slice(s * hd, (s + 1) * hd)
        xh = xbuf_ref[:, sl]
        taps = _causal_taps(xh, range(1, GDN_CONV))
        acc = cw_ref[GDN_CONV - 1:GDN_CONV, sl] * xh[HALO:HALO + tt, :]
        for j in range(1, GDN_CONV):
            acc = acc + cw_ref[GDN_CONV - 1 - j:GDN_CONV - j, sl] * taps[j - 1]
        y = _silu(acc)
        which, head = divmod(s, GDN_HEADS)
        if which < 2:
            y = y * lax.rsqrt(jnp.sum(y * y, axis=-1, keepdims=True) + 1e-12)
        if which == 0:
            y = y * (hd ** -0.5)
        (q_ref, k_ref, v_ref)[which][:, head * hd:(head + 1) * hd] = y
    xbuf_ref[0:HALO, :] = xbuf_ref[tt:tt + HALO, :]

    small = z_ref[:, 4 * GDN_DIM:4 * GDN_DIM + GDN_SMALL]
    beta_ref[...] = jax.nn.sigmoid(small)
    g_ref[...] = -jnp.exp(alog_ref[...]) * _softplus(small + dtb_ref[...])

    m = _pair_masks()
    ng = ng_ref[...]

    def chunk_group(row_slices):
        nh = GDN_HEADS
        chunks = range(len(row_slices))
        hp = [(j, h) for j in chunks for h in range(nh)]
        pp = [(j, p) for j in chunks for p in range(nh // 2)]
        sls = [slice(h * hd, (h + 1) * hd) for h in range(nh)]
        sel = [m["first"], jnp.logical_not(m["first"])]
        gc = [_cumsum_rows(g_ref[rows, :]) for rows in row_slices]
        gct = [x.T for x in gc]
        beta = [beta_ref[rows, :] for rows in row_slices]
        gcol = [gc[j][:, nh + h:nh + h + 1] for j, h in hp]
        grow = [gct[j][nh + h:nh + h + 1, :] for j, h in hp]
        bcol = [beta[j][:, h:h + 1] for j, h in hp]
        q = [q_ref[row_slices[j], sls[h]] for j, h in hp]
        k = [k_ref[row_slices[j], sls[h]] for j, h in hp]
        v = [v_ref[row_slices[j], sls[h]] for j, h in hp]
        n = range(len(hp))
        kb = [k[i] * bcol[i] for i in n]
        egc = [jnp.exp(gcol[i]) for i in n]
        zero = jnp.zeros((c, hd), BF16)
        kbf = [x.astype(BF16) for x in k]
        mm = []
        qk = []
        for j, p in pp:
            i0, i1 = j * nh + 2 * p, j * nh + 2 * p + 1
            gcol2 = jnp.where(m["first"], gcol[i0], gcol[i1])
            grow2 = jnp.concatenate([grow[i0], grow[i1]], axis=1)
            decay = jnp.where(m["incl"], jnp.exp(jnp.where(m["incl"], gcol2 - grow2, 0.0)), 0.0)
            lhs = jnp.concatenate([jnp.concatenate([kb[i0], kb[i1]], axis=1),
                                   jnp.concatenate([q[i0], q[i1]], axis=1)], axis=0).astype(BF16)
            rhs = jnp.concatenate([jnp.concatenate([kbf[i0], zero], axis=1),
                                   jnp.concatenate([zero, kbf[i1]], axis=1)], axis=0)
            kq = _dot_nt(lhs, rhs)
            mm.append(jnp.where(m["strict"], -(kq[0:c, :] * decay), 0.0))
            qk.append((kq[c:2 * c, :] * decay))
        qk2 = [jnp.concatenate([jnp.where(sel[0], x, 0.0), jnp.where(sel[1], x, 0.0)], axis=0).astype(BF16)
               for x in qk]
        t = _pair_inverse_many(mm, m["eye"], m["bd"])
        rhs_sol = [jnp.concatenate(
            [jnp.concatenate([v[i] * bcol[i], kb[i] * egc[i]], axis=1)
             for i in (j * nh + 2 * p, j * nh + 2 * p + 1)], axis=0).astype(BF16)
            for j, p in pp]
        sol = [_dot(jnp.where(sel[i % 2], t[i // 2], 0.0).astype(BF16), rhs_sol[i // 2])
               for i in n]
        qe = [(q[i] * egc[i]).astype(BF16) for i in n]
        g_last = [gcol[i][c - 1:c, :] for i in n]
        kd = [(k[i] * jnp.exp(g_last[i] - gcol[i])).T.astype(BF16) for i in n]
        s_mat = [s_ref[h] for h in range(nh)]
        for j in chunks:
            ids = [j * nh + h for h in range(nh)]
            s_b = [x.astype(BF16) for x in s_mat]
            ws = [_dot(jnp.concatenate([sol[i][:, hd:2 * hd].astype(BF16), qe[i]], axis=0), s_b[h])
                  for h, i in enumerate(ids)]
            vnb = [(sol[i][:, 0:hd] - ws[h][0:c]).astype(BF16) for h, i in enumerate(ids)]
            s_mat = [s_mat[h] * jnp.exp(g_last[i]) + _dot(kd[i], vnb[h]) for h, i in enumerate(ids)]
            qv = [_dot(qk2[j * (nh // 2) + p], jnp.concatenate([vnb[2 * p], vnb[2 * p + 1]], axis=0))
                  for p in range(nh // 2)]
            for h, i in enumerate(ids):
                oo = ws[h][c:2 * c] + qv[h // 2][(h % 2) * c:(h % 2 + 1) * c]
                gate = z_ref[row_slices[j], 3 * GDN_DIM + h * hd:3 * GDN_DIM + (h + 1) * hd]
                o_ref[row_slices[j], sls[h]] = (_rms(oo, ng) * _silu(gate)).astype(o_ref.dtype)
        for h in range(nh):
            s_ref[h] = s_mat[h]

    chunk_group([slice(j * c, (j + 1) * c) for j in range(tt // c)])


def _gdn_mixer(z, conv_w, a_log_pad, dt_bias_pad, norm_g, *, tt=512):
    b, t, _ = z.shape
    const2 = lambda shape: pl.BlockSpec(shape, lambda i, j: (0, 0))
    return pl.pallas_call(
        functools.partial(_gdn_kernel, tt=tt),
        grid=(b, t // tt),
        in_specs=[pl.BlockSpec((None, tt, GDN_Z), lambda i, j: (i, j, 0)),
                  const2((GDN_CONV, 3 * GDN_DIM)), const2((1, GDN_SMALL)), const2((1, GDN_SMALL)),
                  const2((1, GDN_HEAD_DIM))],
        out_specs=pl.BlockSpec((None, tt, GDN_DIM), lambda i, j: (i, j, 0)),
        out_shape=jax.ShapeDtypeStruct((b, t, GDN_DIM), BF16),
        scratch_shapes=[pltpu.VMEM((tt + HALO, 3 * GDN_DIM), F32),
                        pltpu.VMEM((tt, GDN_DIM), F32), pltpu.VMEM((tt, GDN_DIM), F32),
                        pltpu.VMEM((tt, GDN_DIM), F32),
                        pltpu.VMEM((tt, GDN_SMALL), F32), pltpu.VMEM((tt, GDN_SMALL), F32),
                        pltpu.VMEM((GDN_HEADS, GDN_HEAD_DIM, GDN_HEAD_DIM), F32)],
        compiler_params=_params(("parallel", "arbitrary")),
        name="gdn",
    )(z, conv_w, a_log_pad, dt_bias_pad, norm_g.reshape(1, GDN_HEAD_DIM))


def _rwkv_kernel(z_ref, mu_ref, w0_ref, w2_ref, a0_ref, a2_ref, g2_ref, kk_ref, ka_ref, rk_ref,
                 lng_ref, lnb_ref, o_ref,
                 zbuf_ref, r_ref, k_ref, v_ref, n_ref, a_ref, lw_ref, y_ref, p_ref, *, tt):
    c = CHUNK
    w = V7X_LANES
    n_slabs = RWKV_DIM // w

    @pl.when(pl.program_id(1) == 0)
    def _():
        zbuf_ref[0:HALO, :] = jnp.zeros((HALO, RWKV_IN), F32)
        p_ref[...] = jnp.zeros_like(p_ref)

    m = _pair_masks()
    first_head = lax.broadcasted_iota(jnp.int32, (1, w), 1) < RWKV_HEAD_DIM
    not_first = jnp.logical_not(m["first"])

    zbuf_ref[HALO:HALO + tt, :] = z_ref[...]

    def shifted(lo, hi):
        zh = zbuf_ref[:, lo:hi]
        cur = zh[HALO:HALO + tt, :]
        prev, = _causal_taps(zh, (1,))
        return cur + mu_ref[:, lo:hi] * (prev - cur)

    lora = shifted(3 * RWKV_DIM, 3 * RWKV_DIM + DECAY_LORA + ICLR_LORA)
    lora_t = jnp.tanh(lora).astype(BF16)
    lora_b = lora.astype(BF16)
    gate_in = jax.nn.sigmoid(shifted(RWKV_IN - GATE_LORA, RWKV_IN)).astype(BF16)
    for s in range(n_slabs):
        sl = slice(s * w, (s + 1) * w)
        r = shifted(s * w, (s + 1) * w)
        k = shifted(RWKV_DIM + s * w, RWKV_DIM + (s + 1) * w)
        v = shifted(2 * RWKV_DIM + s * w, 2 * RWKV_DIM + (s + 1) * w)
        lw = -RWKV_DECAY_SCALE * jax.nn.sigmoid(w0_ref[:, sl] + _dot(lora_t, w2_ref[:, sl]))
        a = jax.nn.sigmoid(a0_ref[:, sl] + _dot(lora_b, a2_ref[:, sl]))
        kn = k * kk_ref[:, sl]
        kn = kn * lax.rsqrt(_seg_sum(kn * kn, first_head) + 1e-12)
        r_ref[:, sl] = r
        k_ref[:, sl] = k * (1.0 + (a - 1.0) * ka_ref[:, sl])
        v_ref[:, sl] = v
        n_ref[:, sl] = kn
        a_ref[:, sl] = a
        lw_ref[:, sl] = lw
    zbuf_ref[0:HALO, :] = zbuf_ref[tt:tt + HALO, :]

    slabs = range(n_slabs)
    sls = [slice(s * w, (s + 1) * w) for s in slabs]
    bd = lambda x: _block_diag(x, m["bd"])

    def independent_part(probs):
        n = range(len(probs))
        r = [r_ref[rows, sl] for rows, sl in probs]
        k = [k_ref[rows, sl] for rows, sl in probs]
        v = [v_ref[rows, sl] for rows, sl in probs]
        kn = [n_ref[rows, sl] for rows, sl in probs]
        a = [a_ref[rows, sl] for rows, sl in probs]
        lw = [lw_ref[rows, sl] for rows, sl in probs]
        cum = [_cumsum_rows(x) for x in lw]
        b = [kn[i] * a[i] for i in n]
        at = [-kn[i] * jnp.exp(cum[i] - lw[i]) for i in n]
        rt = [(r[i] * jnp.exp(cum[i])).astype(BF16) for i in n]
        e_out = [jnp.exp(-x) for x in cum]
        bt = [b[i] * e_out[i] for i in n]
        kt = [k[i] * e_out[i] for i in n]
        aa = []
        for i in n:
            lhs = jnp.concatenate([at[i].astype(BF16), rt[i]], axis=0)
            rhs = jnp.concatenate(
                [jnp.where(m["first"], bt[i], 0.0), jnp.where(not_first, bt[i], 0.0),
                 jnp.where(m["first"], kt[i], 0.0), jnp.where(not_first, kt[i], 0.0)],
                axis=0).astype(BF16)
            aa.append(_dot_nt(lhs, rhs))
        a_ab = [jnp.where(m["strict"], x[0:c, 0:w], 0.0) for x in aa]
        a_ak = [jnp.where(m["strict"], x[0:c, w:2 * w], 0.0).astype(BF16) for x in aa]
        a_rb = [jnp.where(m["incl"], x[c:2 * c, 0:w], 0.0).astype(BF16) for x in aa]
        a_rk = [jnp.where(m["incl"], x[c:2 * c, w:2 * w], 0.0).astype(BF16) for x in aa]
        v_bd = [bd(x) for x in v]
        av = [_dot(jnp.concatenate([a_ak[i], a_rk[i]], axis=0), v_bd[i]) for i in n]
        t = _pair_inverse_many(a_ab, m["eye"], m["bd"])
        tu = [_dot(t[i].astype(BF16), jnp.concatenate([bd(av[i][0:c]), bd(at[i])], axis=1))
              for i in n]
        lhs_p = [jnp.concatenate([tu[i][:, w:2 * w].astype(BF16), rt[i]], axis=0) for i in n]
        kb_t = []
        d_col = []
        for i in n:
            d_end = jnp.exp(cum[i][c - 1:c, :]) * e_out[i]
            kb_t.append(jnp.concatenate([k[i] * d_end, b[i] * d_end], axis=0).T.astype(BF16))
            d_col.append(jnp.exp(cum[i].T[:, c - 1:c]))
        return dict(v=v, u0=[x[:, 0:w] for x in tu], lhs_p=lhs_p, kb_t=kb_t, d_col=d_col,
                    y0=[x[c:2 * c] for x in av], a_rb=a_rb)

    row_slices = [slice(j * c, (j + 1) * c) for j in range(tt // c)]
    q = independent_part([(rows, sl) for rows in row_slices for sl in sls])

    p_mat = [p_ref[s] for s in slabs]
    for j, rows in enumerate(row_slices):
        ids = [j * n_slabs + s for s in slabs]
        p_b = [x.astype(BF16) for x in p_mat]
        sp = [_dot(q["lhs_p"][i], p_b[s]) for s, i in enumerate(ids)]
        u = [q["u0"][i] + sp[s][0:c] for s, i in enumerate(ids)]
        upd = [_dot(q["kb_t"][i], jnp.concatenate([q["v"][i], u[s]], axis=0).astype(BF16))
               for s, i in enumerate(ids)]
        p_mat = [q["d_col"][i] * p_mat[s] + jnp.where(m["bd"], upd[s], 0.0) for s, i in enumerate(ids)]
        for s, i in enumerate(ids):
            y_ref[rows, sls[s]] = q["y0"][i] + sp[s][c:2 * c] + _dot(q["a_rb"][i], bd(u[s]))
    for s in slabs:
        p_ref[s] = p_mat[s]

    inv_n = 1.0 / RWKV_HEAD_DIM
    for s in range(n_slabs):
        sl = slice(s * w, (s + 1) * w)
        y = y_ref[:, sl]
        mean = _seg_sum(y, first_head) * inv_n
        yc = y - mean
        var = _seg_sum(yc * yc, first_head) * inv_n
        y = yc * lax.rsqrt(var + RWKV_GN_EPS) * lng_ref[:, sl] + lnb_ref[:, sl]
        bonus = _seg_sum(r_ref[:, sl] * k_ref[:, sl] * rk_ref[:, sl], first_head) * v_ref[:, sl]
        g = _dot(gate_in, g2_ref[:, sl])
        o_ref[:, sl] = ((y + bonus) * g).astype(o_ref.dtype)


def _rwkv_mixer(z, mu, w0, w2_pad, a0, a2_pad, g2, k_k, k_a, r_k, ln_g, ln_b, *, tt=512):
    b, t, _ = z.shape
    lora = DECAY_LORA + ICLR_LORA
    const2 = lambda shape: pl.BlockSpec(shape, lambda i, j: (0, 0))
    vec = lambda a: a.reshape(1, RWKV_DIM)
    tile = pltpu.VMEM((tt, RWKV_DIM), F32)
    return pl.pallas_call(
        functools.partial(_rwkv_kernel, tt=tt),
        grid=(b, t // tt),
        in_specs=[pl.BlockSpec((None, tt, RWKV_IN), lambda i, j: (i, j, 0)),
                  const2((1, RWKV_IN)), const2((1, RWKV_DIM)), const2((lora, RWKV_DIM)),
                  const2((1, RWKV_DIM)), const2((lora, RWKV_DIM)), const2((GATE_LORA, RWKV_DIM))]
                 + [const2((1, RWKV_DIM))] * 5,
        out_specs=pl.BlockSpec((None, tt, RWKV_DIM), lambda i, j: (i, j, 0)),
        out_shape=jax.ShapeDtypeStruct((b, t, RWKV_DIM), BF16),
        scratch_shapes=[pltpu.VMEM((tt + HALO, RWKV_IN), F32)] + [tile] * 7
                       + [pltpu.VMEM((RWKV_DIM // V7X_LANES, V7X_LANES, V7X_LANES), F32)],
        compiler_params=_params(("parallel", "arbitrary")),
        name="rwkv7",
    )(z, mu.reshape(1, RWKV_IN), vec(w0), w2_pad, vec(a0), a2_pad, g2, vec(k_k), vec(k_a),
      vec(r_k), vec(ln_g), vec(ln_b))


def _block_diag_weights(w):
    per = (LRU_DIM // 2) // LRU_BLOCK_DIM
    out = jnp.zeros((2, LRU_DIM // 2, LRU_DIM // 2), F32)
    for blk in range(LRU_BLOCKS):
        p, i = divmod(blk, per)
        lo = i * LRU_BLOCK_DIM
        out = out.at[p, lo:lo + LRU_BLOCK_DIM, lo:lo + LRU_BLOCK_DIM].set(w[blk])
    return out.astype(BF16)


def kernel(x, norm1_g, ffn1_wg, ffn1_wu, ffn1_wd, norm_mix_g, w_in, rw_mu, rw_w0, rw_w2, rw_a0, rw_a2, rw_g2, rw_kk, rw_ka, rw_rk, rw_ln_g, rw_ln_b, gd_conv_w, gd_a_log, gd_dt_bias, gd_norm_g, lr_conv_w, lr_conv_b, lr_wa, lr_ba, lr_wx, lr_bx, lr_lam, w_out, norm2_g, ffn2_wg, ffn2_wu, ffn2_wd, final_g):
    b, t, d = x.shape
    n = b * t
    bf = lambda a: a.astype(BF16)
    xs = x.reshape(n, d)
    ffn1 = [_to_bf16(w) for w in (ffn1_wg, ffn1_wu, ffn1_wd)]
    ffn2 = [_to_bf16(w) for w in (ffn2_wg, ffn2_wu, ffn2_wd)]
    for l in range(DEPTH):
        xs = _ffn(xs, norm1_g[l], *ffn1, l)

        w_l = w_in[l]
        w_rw = bf(w_l[:, :RWKV_IN])
        w_gd = bf(jnp.pad(w_l[:, RWKV_IN:RWKV_IN + GDN_IN], ((0, 0), (0, GDN_Z - GDN_IN))))
        w_lr = bf(w_l[:, RWKV_IN + GDN_IN:])
        z_rw, y_lr = _proj_rwkv_lru(xs, norm_mix_g[l], w_rw, w_lr, lr_conv_w[l], lr_conv_b[l],
                                    _block_diag_weights(lr_wa[l]), lr_ba[l],
                                    _block_diag_weights(lr_wx[l]), lr_bx[l], lr_lam[l], seq_len=t)
        z_gd, = _norm_proj(xs, norm_mix_g[l], [w_gd], name="in_proj_gdn")
        z_rw = z_rw.reshape(b, t, RWKV_IN)
        z_gd = z_gd.reshape(b, t, GDN_Z)

        w2_pad = bf(jnp.concatenate([rw_w2[l], jnp.zeros((ICLR_LORA, RWKV_DIM), F32)], axis=0))
        a2_pad = bf(jnp.concatenate([jnp.zeros((DECAY_LORA, RWKV_DIM), F32), rw_a2[l]], axis=0))
        y_rw = _rwkv_mixer(z_rw, rw_mu[l], rw_w0[l], w2_pad, rw_a0[l], a2_pad, bf(rw_g2[l]),
                           rw_kk[l], rw_ka[l], rw_rk[l].reshape(RWKV_DIM), rw_ln_g[l], rw_ln_b[l])

        lane_pad = lambda a: jnp.pad(a, (GDN_HEADS, GDN_SMALL - 2 * GDN_HEADS)).reshape(1, GDN_SMALL)
        y_gd = _gdn_mixer(z_gd, gd_conv_w[l], lane_pad(gd_a_log[l]), lane_pad(gd_dt_bias[l]),
                          gd_norm_g[l])

        w_o = bf(w_out[l])
        xs = _out_proj(xs, y_rw.reshape(n, RWKV_DIM), y_gd.reshape(n, GDN_DIM),
                       y_lr.reshape(n, LRU_DIM), w_o[:RWKV_DIM], w_o[RWKV_DIM:RWKV_DIM + GDN_DIM],
                       w_o[RWKV_DIM + GDN_DIM:])
        last = l == DEPTH - 1
        xs = _ffn(xs, norm2_g[l], *ffn2, l, final_g if last else None)
    return xs.reshape(b, t, d)
```
